```python
import math
import jax, jax.numpy as jnp
from jax import lax
import numpy as np

D_MODEL = 4096
BATCH = 2
SEQ = 8192
DEPTH = 2

N_META = 16
NORM_EPS = 1e-6
D_FF = 2 * D_MODEL
SSM_HEAD_DIM = 64
SSM_HEADS = D_MODEL // SSM_HEAD_DIM
D_SSM = SSM_HEADS * SSM_HEAD_DIM
SSM_GROUPS = 8
SSM_STATE = 128
SSM_CONV = 4
SSM_CHUNK = 128
SSM_CONV_DIM = D_SSM + 2 * SSM_GROUPS * SSM_STATE
GDN_HEAD_DIM = 128
GDN_QK_HEADS = D_MODEL // 256
GDN_V_HEADS = 2 * GDN_QK_HEADS
GDN_CONV = 4
GDN_CHUNK = 64
D_GDN_QK = GDN_QK_HEADS * GDN_HEAD_DIM
D_GDN_V = GDN_V_HEADS * GDN_HEAD_DIM
GDN_CONV_DIM = 2 * D_GDN_QK + D_GDN_V
IN_SIZES = (D_SSM, SSM_CONV_DIM, SSM_HEADS, GDN_CONV_DIM, D_GDN_V, GDN_V_HEADS, GDN_V_HEADS, 2 * D_MODEL)
N_IN = sum(IN_SIZES)
IN_SPLITS = [int(v) for v in np.cumsum(IN_SIZES)[:-1]]

kernel_name = "hybrid_ssd_gdn_gated_macaron"


def rms_norm(x, g):
    xf = x.astype(jnp.float32)
    y = xf * lax.rsqrt(jnp.mean(xf * xf, axis=-1, keepdims=True) + NORM_EPS)
    return (y * g.astype(jnp.float32)).astype(x.dtype)


def l2_norm(x):
    xf = x.astype(jnp.float32)
    return (xf * lax.rsqrt(jnp.sum(xf * xf, axis=-1, keepdims=True) + NORM_EPS)).astype(x.dtype)


def swiglu_ffn(x, w_gate_up, w_down):
    g, u = jnp.split(x @ w_gate_up, 2, axis=-1)
    return (jax.nn.silu(g) * u) @ w_down


def causal_dwconv(x, w):
    k = w.shape[0]
    return lax.conv_general_dilated(
        x, w[:, None, :].astype(x.dtype), window_strides=(1,), padding=[(k - 1, 0)],
        dimension_numbers=("NWC", "WIO", "NWC"), feature_group_count=x.shape[-1])


def to_chunks(t, pad, chunk):
    t = jnp.pad(t.astype(jnp.float32), [(0, 0), (pad, 0)] + [(0, 0)] * (t.ndim - 2))
    nc = t.shape[1] // chunk
    return jnp.moveaxis(t.reshape((t.shape[0], nc, chunk) + t.shape[2:]), 1, 0)


def from_chunks(y, pad):
    y = jnp.moveaxis(y, 0, 1)
    y = y.reshape((y.shape[0], -1) + y.shape[3:])
    return y[:, pad:]


def ssd_chunked_scan(x, dt, a, bm, cm):
    bsz, seqlen, nh, hp = x.shape
    ng, ns = bm.shape[2], bm.shape[3]
    hg = nh // ng
    pad = (-seqlen) % SSM_CHUNK
    xc = to_chunks(x, pad, SSM_CHUNK)
    xc = xc.reshape(xc.shape[:3] + (ng, hg, hp))
    dtc = to_chunks(dt, pad, SSM_CHUNK)
    dtc = dtc.reshape(dtc.shape[:3] + (ng, hg))
    bc = to_chunks(bm, pad, SSM_CHUNK)
    cc = to_chunks(cm, pad, SSM_CHUNK)
    a_g = a.astype(jnp.float32).reshape(ng, hg)
    causal = jnp.tril(jnp.ones((SSM_CHUNK, SSM_CHUNK), dtype=bool))[None, :, :, None, None]

    def step(state, inp):
        xk, dtk, bk, ck = inp
        la = jnp.cumsum(dtk * a_g, axis=1)
        seg = la[:, :, None] - la[:, None]
        decay = jnp.exp(jnp.where(causal, seg, -jnp.inf))
        cb = jnp.einsum("bign,bjgn->bijg", ck, bk)
        y_intra = jnp.einsum("bijg,bijgh,bjgh,bjghp->bighp", cb, decay, dtk, xk)
        y_inter = jnp.einsum("bign,bghpn->bighp", ck, state) * jnp.exp(la)[..., None]
        to_end = jnp.exp(la[:, -1:] - la) * dtk
        new_state = (state * jnp.exp(la[:, -1])[..., None, None]
                     + jnp.einsum("bjgn,bjgh,bjghp->bghpn", bk, to_end, xk))
        return new_state, y_intra + y_inter

    state0 = jnp.zeros((bsz, ng, hg, hp, ns), jnp.float32)
    _, ys = lax.scan(step, state0, (xc, dtc, bc, cc))
    y = from_chunks(ys, pad)
    return y.reshape(bsz, seqlen, nh, hp).astype(x.dtype)


def gated_delta_chunked(q, k, v, beta, g):
    bsz, seqlen, nh, dk = q.shape
    dv = v.shape[-1]
    pad = (-seqlen) % GDN_CHUNK
    qc, kc, vc = (to_chunks(t, pad, GDN_CHUNK) for t in (q, k, v))
    bc, gcn = (to_chunks(t, pad, GDN_CHUNK) for t in (beta, g))
    eye = jnp.eye(GDN_CHUNK, dtype=jnp.float32)
    incl = jnp.tril(jnp.ones((GDN_CHUNK, GDN_CHUNK), dtype=bool))
    strict = jnp.tril(jnp.ones((GDN_CHUNK, GDN_CHUNK), dtype=bool), -1)

    def step(state, inp):
        qk_, kk, vk, bk, gk = (jnp.swapaxes(t, 1, 2) for t in inp)
        gc = jnp.cumsum(gk, axis=-1)
        gam = jnp.exp(jnp.where(incl, gc[..., :, None] - gc[..., None, :], -jnp.inf))
        kkt = jnp.einsum("bhid,bhjd->bhij", kk, kk)
        m = jnp.where(strict, kkt * gam * bk[..., :, None], 0.0)
        rhs = jnp.concatenate([vk * bk[..., None], kk * (bk * jnp.exp(gc))[..., None]], axis=-1)
        sol = lax.linalg.triangular_solve(eye + m, rhs, left_side=True, lower=True)
        u, w = sol[..., :dv], sol[..., dv:]
        v_new = u - jnp.einsum("bhik,bhkv->bhiv", w, state)
        qkt = jnp.einsum("bhid,bhjd->bhij", qk_, kk) * gam
        o = (jnp.einsum("bhik,bhkv->bhiv", qk_ * jnp.exp(gc)[..., None], state)
             + jnp.einsum("bhij,bhjv->bhiv", qkt, v_new))
        g_last = gc[..., -1]
        new_state = (state * jnp.exp(g_last)[..., None, None]
                     + jnp.einsum("bhjk,bhjv->bhkv", kk * jnp.exp(g_last[..., None] - gc)[..., None], v_new))
        return new_state, jnp.swapaxes(o, 1, 2)

    state0 = jnp.zeros((bsz, nh, dk, dv), jnp.float32)
    _, os_ = lax.scan(step, state0, (qc, kc, vc, bc, gcn))
    return from_chunks(os_, pad).astype(v.dtype)


def hybrid_mixer(u, w_in, ssm_conv_w, ssm_conv_b, ssm_dt_bias, ssm_a_log, ssm_d, ssm_norm, ssm_w_out,
                 gdn_conv_w, gdn_dt_bias, gdn_a_log, gdn_norm, gdn_w_out, gate_b, w_o):
    bsz, seqlen, _ = u.shape
    z_m, xbc, dt_raw, qkv, z_g, b_raw, a_raw, gate_raw = jnp.split(u @ w_in, IN_SPLITS, axis=-1)

    xbc = jax.nn.silu(causal_dwconv(xbc, ssm_conv_w) + ssm_conv_b)
    xs, bm, cm = jnp.split(xbc, [D_SSM, D_SSM + SSM_GROUPS * SSM_STATE], axis=-1)
    xs = xs.reshape(bsz, seqlen, SSM_HEADS, SSM_HEAD_DIM)
    dt = jax.nn.softplus(dt_raw + ssm_dt_bias)
    a = -jnp.exp(ssm_a_log)
    y = ssd_chunked_scan(xs, dt, a,
                         bm.reshape(bsz, seqlen, SSM_GROUPS, SSM_STATE),
                         cm.reshape(bsz, seqlen, SSM_GROUPS, SSM_STATE))
    y = (y + ssm_d[:, None] * xs).reshape(bsz, seqlen, D_SSM) * jax.nn.silu(z_m)
    y = rms_norm(y.reshape(bsz, seqlen, SSM_GROUPS, D_SSM // SSM_GROUPS),
                 ssm_norm.reshape(SSM_GROUPS, D_SSM // SSM_GROUPS)).reshape(bsz, seqlen, D_SSM)
    branch_m = y @ ssm_w_out

    qkv = jax.nn.silu(causal_dwconv(qkv, gdn_conv_w))
    q, k, v = jnp.split(qkv, [D_GDN_QK, 2 * D_GDN_QK], axis=-1)
    q = l2_norm(q.reshape(bsz, seqlen, GDN_QK_HEADS, GDN_HEAD_DIM)) * (GDN_HEAD_DIM ** -0.5)
    k = l2_norm(k.reshape(bsz, seqlen, GDN_QK_HEADS, GDN_HEAD_DIM))
    rep = GDN_V_HEADS // GDN_QK_HEADS
    q = jnp.repeat(q, rep, axis=2)
    k = jnp.repeat(k, rep, axis=2)
    v = v.reshape(bsz, seqlen, GDN_V_HEADS, GDN_HEAD_DIM)
    beta = jax.nn.sigmoid(b_raw)
    g = -jnp.exp(gdn_a_log) * jax.nn.softplus(a_raw + gdn_dt_bias)
    o = gated_delta_chunked(q, k, v, beta, g)
    o = rms_norm(o, gdn_norm) * jax.nn.silu(z_g.reshape(bsz, seqlen, GDN_V_HEADS, GDN_HEAD_DIM))
    branch_g = o.reshape(bsz, seqlen, D_GDN_V) @ gdn_w_out

    gate_m, gate_g = jnp.split(jax.nn.sigmoid(gate_raw + gate_b), 2, axis=-1)
    return (gate_m * branch_m + gate_g * branch_g) @ w_o


def setup_inputs(seed: int = 0) -> dict:
    key = jax.random.key(seed)
    ks = iter(jax.random.split(key, 32))
    f32 = jnp.float32

    def nrm(shape, fan_in):
        return jax.random.normal(next(ks), shape, f32) * (fan_in ** -0.5)

    def gain(shape):
        return 1.0 + 0.02 * jax.random.normal(next(ks), shape, f32)

    def small(shape):
        return 0.01 * jax.random.normal(next(ks), shape, f32)

    def dt_bias(shape):
        uu = jax.random.uniform(next(ks), shape, f32)
        dt = jnp.exp(uu * (math.log(0.1) - math.log(1e-3)) + math.log(1e-3))
        return dt + jnp.log(-jnp.expm1(-dt))

    def a_log(shape):
        return jnp.log(jax.random.uniform(next(ks), shape, f32, 1.0, 16.0))

    L = DEPTH
    return {
        "x": jax.random.normal(next(ks), (BATCH, SEQ, D_MODEL), f32),
        "meta_tokens": jax.random.normal(next(ks), (N_META, D_MODEL), f32),
        "ffn1_norm": gain((L, D_MODEL)),
        "ffn1_w_gate_up": nrm((L, D_MODEL, 2 * D_FF), D_MODEL),
        "ffn1_w_down": nrm((L, D_FF, D_MODEL), D_FF),
        "mix_norm": gain((L, D_MODEL)),
        "w_in": nrm((L, D_MODEL, N_IN), D_MODEL),
        "ssm_conv_w": nrm((L, SSM_CONV, SSM_CONV_DIM), SSM_CONV),
        "ssm_conv_b": small((L, SSM_CONV_DIM)),
        "ssm_dt_bias": dt_bias((L, SSM_HEADS)),
        "ssm_a_log": a_log((L, SSM_HEADS)),
        "ssm_d": gain((L, SSM_HEADS)),
        "ssm_norm": gain((L, D_SSM)),
        "ssm_w_out": nrm((L, D_SSM, D_MODEL), D_SSM),
        "gdn_conv_w": nrm((L, GDN_CONV, GDN_CONV_DIM), GDN_CONV),
        "gdn_dt_bias": dt_bias((L, GDN_V_HEADS)),
        "gdn_a_log": a_log((L, GDN_V_HEADS)),
        "gdn_norm": gain((L, GDN_HEAD_DIM)),
        "gdn_w_out": nrm((L, D_GDN_V, D_MODEL), D_GDN_V),
        "gate_b": small((L, 2 * D_MODEL)),
        "w_o": nrm((L, D_MODEL, D_MODEL), D_MODEL),
        "ffn2_norm": gain((L, D_MODEL)),
        "ffn2_w_gate_up": nrm((L, D_MODEL, 2 * D_FF), D_MODEL),
        "ffn2_w_down": nrm((L, D_FF, D_MODEL), D_FF),
        "final_norm": gain((D_MODEL,)),
    }


def reference(x, meta_tokens, ffn1_norm, ffn1_w_gate_up, ffn1_w_down, mix_norm, w_in,
              ssm_conv_w, ssm_conv_b, ssm_dt_bias, ssm_a_log, ssm_d, ssm_norm, ssm_w_out,
              gdn_conv_w, gdn_dt_bias, gdn_a_log, gdn_norm, gdn_w_out, gate_b, w_o,
              ffn2_norm, ffn2_w_gate_up, ffn2_w_down, final_norm):
    bsz = x.shape[0]
    meta = jnp.broadcast_to(meta_tokens[None].astype(x.dtype), (bsz, N_META, D_MODEL))
    h = jnp.concatenate([meta, x], axis=1)
    for i in range(DEPTH):
        h = h + 0.5 * swiglu_ffn(rms_norm(h, ffn1_norm[i]), ffn1_w_gate_up[i], ffn1_w_down[i])
        h = h + hybrid_mixer(rms_norm(h, mix_norm[i]), w_in[i],
                             ssm_conv_w[i], ssm_conv_b[i], ssm_dt_bias[i], ssm_a_log[i], ssm_d[i],
                             ssm_norm[i], ssm_w_out[i],
                             gdn_conv_w[i], gdn_dt_bias[i], gdn_a_log[i], gdn_norm[i], gdn_w_out[i],
                             gate_b[i], w_o[i])
        h = h + 0.5 * swiglu_ffn(rms_norm(h, ffn2_norm[i]), ffn2_w_gate_up[i], ffn2_w_down[i])
    return rms_norm(h, final_norm)[:, N_META:]
```

```python
import functools

import jax
import jax.numpy as jnp
from jax import lax
from jax.experimental import pallas as pl
from jax.experimental.pallas import tpu as pltpu

F32 = jnp.float32
BF16 = jnp.bfloat16

D_MODEL = 4096
N_META = 16
NORM_EPS = 1e-6
D_FF = 2 * D_MODEL
SSM_HEADS = 64
SSM_HEAD_DIM = 64
SSM_GROUPS = 8
SSM_STATE = 128
SSM_CONV = 4
SSM_CHUNK = 128
D_SSM = SSM_HEADS * SSM_HEAD_DIM
D_BC = 2 * SSM_GROUPS * SSM_STATE
GDN_DK = 128
GDN_QK_HEADS = 16
GDN_V_HEADS = 32
GDN_CONV = 4
GDN_CHUNK = 64
D_QK = GDN_QK_HEADS * GDN_DK
D_V = GDN_V_HEADS * GDN_DK

LANES = 128
SUBLANES = 8
VMEM_CAP = 60 * 1024 * 1024

PROJ_ZM, PROJ_XS, PROJ_ZG, PROJ_V, PROJ_GM, PROJ_GG = 0, 4096, 8192, 12288, 16384, 20480
PROJ_BC, PROJ_Q, PROJ_K = 24576, 26624, 28672
N_PROJ = 30720


def _cparams(sem, vmem_bytes):
    return pltpu.CompilerParams(dimension_semantics=sem, vmem_limit_bytes=min(int(vmem_bytes), VMEM_CAP))


def _pick_tile(n, candidates):
    for c in candidates:
        if n % c == 0:
            return c
    raise ValueError(f"no tile for {n} in {candidates}")


def _silu(x):
    return x * jax.nn.sigmoid(x)


def _softplus(x):
    return jnp.maximum(x, 0.0) + jnp.log(1.0 + jnp.exp(-jnp.abs(x)))


def _split_bf16(x, passes):
    parts = []
    r = x
    for p in range(passes):
        b = r.astype(BF16)
        parts.append(b)
        if p + 1 < passes:
            r = r - b.astype(F32)
    return parts


def _dot_split_rhs(a_bf16, x, passes):
    acc = None
    for p in _split_bf16(x, passes):
        d = jnp.dot(a_bf16, p, preferred_element_type=F32)
        acc = d if acc is None else acc + d
    return acc


def _dot_split_lhs(x, e_bf16, passes):
    acc = None
    for p in _split_bf16(x, passes):
        d = jnp.dot(p, e_bf16, preferred_element_type=F32)
        acc = d if acc is None else acc + d
    return acc


def _dot_nt(a, b):
    return lax.dot_general(a, b, (((1,), (1,)), ((), ())), preferred_element_type=F32)


def _bdot(a, b):
    return jnp.dot(a.astype(BF16), b.astype(BF16), preferred_element_type=F32)


def _rmsnorm_kernel(h_ref, g_ref, o_ref):
    x = h_ref[...]
    ms = jnp.mean(x * x, axis=-1, keepdims=True)
    o_ref[...] = (x * lax.rsqrt(ms + NORM_EPS) * g_ref[...]).astype(o_ref.dtype)


def _rmsnorm(h, gain):
    t, d = h.shape
    tm = _pick_tile(t, (256, 128))
    return pl.pallas_call(
        _rmsnorm_kernel,
        grid=(t // tm,),
        in_specs=[pl.BlockSpec((tm, d), lambda i: (i, 0)), pl.BlockSpec((1, d), lambda i: (0, 0))],
        out_specs=pl.BlockSpec((tm, d), lambda i: (i, 0)),
        out_shape=jax.ShapeDtypeStruct((t, d), BF16),
        compiler_params=_cparams(("parallel",), 4 * tm * d * (4 + 2) + (4 << 20)),
        name="rmsnorm",
    )(h, gain.reshape(1, d))


def _ffn_up_kernel(x_ref, wg_ref, wu_ref, o_ref):
    x = x_ref[...]
    g = jnp.dot(x, wg_ref[...], preferred_element_type=F32)
    u = jnp.dot(x, wu_ref[...], preferred_element_type=F32)
    o_ref[...] = (_silu(g) * u * 0.5).astype(o_ref.dtype)


def _ffn_up(xn, w_gate_up):
    t, d = xn.shape
    tm = _pick_tile(t, (1280, 640, 256, 128))
    tn = 512
    nj = D_FF // tn
    vmem = 2 * (tm * d * 2 + 2 * d * tn * 2 + tm * tn * 2) + 4 * tm * tn * 4 + (4 << 20)
    return pl.pallas_call(
        _ffn_up_kernel,
        grid=(t // tm, nj),
        in_specs=[
            pl.BlockSpec((tm, d), lambda i, j: (i, 0)),
            pl.BlockSpec((d, tn), lambda i, j: (0, j)),
            pl.BlockSpec((d, tn), lambda i, j: (0, j + nj)),
        ],
        out_specs=pl.BlockSpec((tm, tn), lambda i, j: (i, j)),
        out_shape=jax.ShapeDtypeStruct((t, D_FF), BF16),
        compiler_params=_cparams(("parallel", "arbitrary"), vmem),
        name="ffn_up",
    )(xn, w_gate_up, w_gate_up)


def _mm_res_kernel(a_ref, w_ref, h_ref, o_ref, *, nk):
    p = jnp.dot(a_ref[...], w_ref[...], preferred_element_type=F32)
    if nk == 1:
        o_ref[...] = h_ref[...] + p
    else:
        k = pl.program_id(2)

        @pl.when(k == 0)
        def _():
            o_ref[...] = h_ref[...] + p

        @pl.when(k > 0)
        def _():
            o_ref[...] += p


def _mm_res(a, w, h):
    t, kdim = a.shape
    n = w.shape[1]
    tm = _pick_tile(t, (1280, 640, 256, 128))
    tn = 512
    tk = 4096
    nk = kdim // tk
    vmem = 2 * (tm * tk * 2 + tk * tn * 2 + 2 * tm * tn * 4) + 2 * tm * tn * 4 + (4 << 20)
    return pl.pallas_call(
        functools.partial(_mm_res_kernel, nk=nk),
        grid=(t // tm, n // tn, nk),
        in_specs=[
            pl.BlockSpec((tm, tk), lambda i, j, k: (i, k)),
            pl.BlockSpec((tk, tn), lambda i, j, k: (k, j)),
            pl.BlockSpec((tm, tn), lambda i, j, k: (i, j)),
        ],
        out_specs=pl.BlockSpec((tm, tn), lambda i, j, k: (i, j)),
        out_shape=jax.ShapeDtypeStruct((t, n), F32),
        compiler_params=_cparams(("parallel", "parallel", "arbitrary"), vmem),
        name="mm_res",
    )(a, w, h)


def _inproj_kernel(x_ref, w_ref, ws_ref, o_ref, os_ref):
    x = x_ref[...]
    o_ref[...] = jnp.dot(x, w_ref[...], preferred_element_type=F32).astype(o_ref.dtype)

    @pl.when(pl.program_id(1) == 0)
    def _():
        os_ref[...] = jnp.dot(x, ws_ref[...], preferred_element_type=F32)


def _inproj(xn, w_big, w_small):
    t, d = xn.shape
    tm = _pick_tile(t, (1280, 640, 256, 128))
    tn = 1024
    vmem = 2 * (tm * d * 2 + d * tn * 2 + tm * tn * 2 + d * LANES * 2 + tm * LANES * 4) + 2 * tm * tn * 4 + (4 << 20)
    return pl.pallas_call(
        _inproj_kernel,
        grid=(t // tm, N_PROJ // tn),
        in_specs=[
            pl.BlockSpec((tm, d), lambda i, j: (i, 0)),
            pl.BlockSpec((d, tn), lambda i, j: (0, j)),
            pl.BlockSpec((d, LANES), lambda i, j: (0, 0)),
        ],
        out_specs=[
            pl.BlockSpec((tm, tn), lambda i, j: (i, j)),
            pl.BlockSpec((tm, LANES), lambda i, j: (i, 0)),
        ],
        out_shape=[jax.ShapeDtypeStruct((t, N_PROJ), BF16), jax.ShapeDtypeStruct((t, LANES), F32)],
        compiler_params=_cparams(("parallel", "arbitrary"), vmem),
        name="inproj",
    )(xn, w_big, w_small)


def _branch_kernel(y_ref, o_ref, wm_ref, wg_ref, gm_ref, gg_ref, bm_ref, bg_ref, out_ref):
    br_m = jnp.dot(y_ref[...], wm_ref[...], preferred_element_type=F32)
    br_g = jnp.dot(o_ref[...], wg_ref[...], preferred_element_type=F32)
    gate_m = jax.nn.sigmoid(gm_ref[...].astype(F32) + bm_ref[...])
    gate_g = jax.nn.sigmoid(gg_ref[...].astype(F32) + bg_ref[...])
    out_ref[...] = (gate_m * br_m + gate_g * br_g).astype(out_ref.dtype)


def _branch_merge(y, o, w_m, w_g, proj, gate_b):
    t, d = y.shape
    tm = _pick_tile(t, (640, 256, 128))
    tn = 512
    jm, jg = PROJ_GM // tn, PROJ_GG // tn
    nj = d // tn
    vmem = 2 * (2 * tm * d * 2 + 2 * d * tn * 2 + 3 * tm * tn * 2) + 6 * tm * tn * 4 + (4 << 20)
    gb = gate_b.reshape(1, 2 * d)
    return pl.pallas_call(
        _branch_kernel,
        grid=(t // tm, nj),
        in_specs=[
            pl.BlockSpec((tm, d), lambda i, j: (i, 0)),
            pl.BlockSpec((tm, d), lambda i, j: (i, 0)),
            pl.BlockSpec((d, tn), lambda i, j: (0, j)),
            pl.BlockSpec((d, tn), lambda i, j: (0, j)),
            pl.BlockSpec((tm, tn), lambda i, j: (i, j + jm)),
            pl.BlockSpec((tm, tn), lambda i, j: (i, j + jg)),
            pl.BlockSpec((1, tn), lambda i, j: (0, j)),
            pl.BlockSpec((1, tn), lambda i, j: (0, j + nj)),
        ],
        out_specs=pl.BlockSpec((tm, tn), lambda i, j: (i, j)),
        out_shape=jax.ShapeDtypeStruct((t, d), BF16),
        compiler_params=_cparams(("parallel", "arbitrary"), vmem),
        name="branch_merge",
    )(y, o, w_m, w_g, proj, proj, gb, gb)


def _conv_silu(x_ref, pad_ref, w_ref, b_ref, rows):
    pad_ref[SUBLANES:SUBLANES + rows, :] = x_ref[...].astype(F32)
    acc = w_ref[3:4, :] * pad_ref[SUBLANES:SUBLANES + rows, :]
    for k in range(3):
        off = SUBLANES - 3 + k
        acc = acc + w_ref[k:k + 1, :] * pad_ref[off:off + rows, :]
    if b_ref is not None:
        acc = acc + b_ref[...]
    pad_ref[0:SUBLANES, :] = pad_ref[rows:rows + SUBLANES, :]
    return _silu(acc)


def _ssd_kernel(xs_ref, bc_ref, z_ref, sm_ref, cwx_ref, cbx_ref, cwbc_ref, cbbc_ref, dtb_ref, alog_ref,
                drep_ref, nw_ref, e64_ref, e128_ref,
                y_ref,
                st_ref, xpad_ref, bcpad_ref, xact_ref, xbf_ref, xw_ref, bcact_ref, larep_ref, exprep_ref,
                lat_ref, dtt_ref, yacc_ref):
    q = SSM_CHUNK
    hg = SSM_HEADS // SSM_GROUPS
    gw = hg * SSM_HEAD_DIM

    @pl.when(pl.program_id(1) == 0)
    def _():
        st_ref[...] = jnp.zeros_like(st_ref)
        xpad_ref[0:SUBLANES, :] = jnp.zeros((SUBLANES, D_SSM), F32)
        bcpad_ref[0:SUBLANES, :] = jnp.zeros((SUBLANES, D_BC), F32)

    xact = _conv_silu(xs_ref, xpad_ref, cwx_ref, cbx_ref, q)
    xact_ref[...] = xact
    xbf_ref[...] = xact.astype(BF16)
    bcact_ref[...] = _conv_silu(bc_ref, bcpad_ref, cwbc_ref, cbbc_ref, q)

    row = lax.broadcasted_iota(jnp.int32, (q, q), 0)
    col = lax.broadcasted_iota(jnp.int32, (q, q), 1)
    causal = row >= col
    tril = jnp.where(causal, 1.0, 0.0).astype(BF16)

    head_lane = col < SSM_HEADS
    dt = jnp.where(head_lane, _softplus(sm_ref[...] + dtb_ref[...]), 0.0)
    a = -jnp.exp(alog_ref[...])
    la = _dot_split_rhs(tril, dt * a, 3)
    la_last = la[q - 1:q, :]
    larep_ref[...] = _dot_split_lhs(la, e128_ref[...], 3)
    exprep_ref[...] = _dot_split_lhs(jnp.exp(la), e64_ref[...], 2)
    to_end = jnp.exp(la_last - la) * dt
    xw_ref[...] = (xact * _dot_split_lhs(to_end, e64_ref[...], 2)).astype(BF16)
    lat_ref[...] = la.T
    dtt_ref[...] = dt.T

    def group_body(g, carry):
        c0 = pl.multiple_of(g * SSM_STATE, SSM_STATE)
        x0 = pl.multiple_of(g * gw, gw)
        bg = bcact_ref[:, pl.ds(c0, SSM_STATE)]
        cg = bcact_ref[:, pl.ds(SSM_GROUPS * SSM_STATE + c0, SSM_STATE)].astype(BF16)
        cb = _dot_nt(cg, bg.astype(BF16))
        st_g = st_ref[:, pl.ds(x0, gw)]
        dec_g = exprep_ref[:, pl.ds(x0, gw)]
        y_inter = jnp.dot(cg, st_g.astype(BF16), preferred_element_type=F32) * dec_g
        st_ref[:, pl.ds(x0, gw)] = st_g * dec_g[q - 1:q, :] + jnp.dot(
            bg.T.astype(BF16), xw_ref[:, pl.ds(x0, gw)], preferred_element_type=F32)
        for pp in range(hg // 2):
            xp0 = pl.multiple_of(x0 + pp * LANES, LANES)
            xpair = xbf_ref[:, pl.ds(xp0, LANES)]
            res = []
            for e in range(2):
                h = g * hg + pp * 2 + e
                seg = larep_ref[:, pl.ds(pl.multiple_of(h * q, q), q)] - lat_ref[pl.ds(h, 1), :]
                dec = jnp.where(causal, jnp.exp(seg), 0.0)
                lmat = (cb * dec * dtt_ref[pl.ds(h, 1), :]).astype(BF16)
                res.append(jnp.dot(lmat, xpair, preferred_element_type=F32))
            y_intra = jnp.where(col < SSM_HEAD_DIM, res[0], res[1])
            yacc_ref[:, pl.ds(xp0, LANES)] = y_intra + y_inter[:, pp * LANES:(pp + 1) * LANES]
        return carry

    lax.fori_loop(0, SSM_GROUPS, group_body, 0)

    z = z_ref[...].astype(F32)
    y = (yacc_ref[...] + drep_ref[...] * xact_ref[...]) * _silu(z)
    for g in range(SSM_GROUPS):
        yg = y[:, g * gw:(g + 1) * gw]
        ms = jnp.mean(yg * yg, axis=-1, keepdims=True)
        y_ref[:, g * gw:(g + 1) * gw] = (yg * lax.rsqrt(ms + NORM_EPS) * nw_ref[:, g * gw:(g + 1) * gw]).astype(y_ref.dtype)


def _ssd(proj, small, conv_w, conv_b, dt_bias, a_log, d_skip, norm_w, bsz, tb):
    q = SSM_CHUNK
    nc = tb // q
    t = bsz * tb
    pad = LANES - SSM_HEADS
    dtb = jnp.pad(dt_bias, (0, pad)).reshape(1, LANES)
    alog = jnp.pad(a_log, (0, pad)).reshape(1, LANES)
    drep = jnp.repeat(d_skip, SSM_HEAD_DIM).reshape(1, D_SSM)
    r = jnp.arange(LANES)[:, None]
    e64 = (jnp.arange(D_SSM)[None, :] // SSM_HEAD_DIM == r).astype(BF16)
    e128 = (jnp.arange(SSM_HEADS * q)[None, :] // q == r).astype(BF16)
    const = lambda b, c: (0, 0)
    rowblk = lambda off: (lambda b, c: (b * nc + c, off))
    return pl.pallas_call(
        _ssd_kernel,
        grid=(bsz, nc),
        in_specs=[
            pl.BlockSpec((q, D_SSM), rowblk(PROJ_XS // D_SSM)),
            pl.BlockSpec((q, D_BC), rowblk(PROJ_BC // D_BC)),
            pl.BlockSpec((q, D_SSM), rowblk(PROJ_ZM // D_SSM)),
            pl.BlockSpec((q, LANES), rowblk(0)),
            pl.BlockSpec((SSM_CONV, D_SSM), const),
            pl.BlockSpec((1, D_SSM), const),
            pl.BlockSpec((SSM_CONV, D_BC), const),
            pl.BlockSpec((1, D_BC), const),
            pl.BlockSpec((1, LANES), const),
            pl.BlockSpec((1, LANES), const),
            pl.BlockSpec((1, D_SSM), const),
            pl.BlockSpec((1, D_SSM), const),
            pl.BlockSpec((LANES, D_SSM), const),
            pl.BlockSpec((LANES, SSM_HEADS * q), const),
        ],
        out_specs=pl.BlockSpec((q, D_SSM), rowblk(0)),
        out_shape=jax.ShapeDtypeStruct((t, D_SSM), BF16),
        scratch_shapes=[
            pltpu.VMEM((SSM_STATE, D_SSM), F32),
            pltpu.VMEM((q + SUBLANES, D_SSM), F32),
            pltpu.VMEM((q + SUBLANES, D_BC), F32),
            pltpu.VMEM((q, D_SSM), F32),
            pltpu.VMEM((q, D_SSM), BF16),
            pltpu.VMEM((q, D_SSM), BF16),
            pltpu.VMEM((q, D_BC), F32),
            pltpu.VMEM((q, SSM_HEADS * q), F32),
            pltpu.VMEM((q, D_SSM), F32),
            pltpu.VMEM((LANES, q), F32),
            pltpu.VMEM((LANES, q), F32),
            pltpu.VMEM((q, D_SSM), F32),
        ],
        compiler_params=_cparams(("parallel", "arbitrary"), 56 << 20),
        name="ssd_scan",
    )(proj, proj, proj, small, conv_w[:, :D_SSM], conv_b[:D_SSM].reshape(1, D_SSM), conv_w[:, D_SSM:],
      conv_b[D_SSM:].reshape(1, D_BC), dtb, alog, drep, norm_w.reshape(1, D_SSM), e64, e128)


def _unit_lower_inverse_minus_eye(m, row, col):
    blk16 = (row ^ col) < 16
    blk32 = (row ^ col) < 32
    md = jnp.where(blk16, m, 0.0)
    n = -md
    p = md
    for _ in range(3):
        p = _bdot(p, p)
        n = n + p + _bdot(n, p)
    for off in (jnp.where(blk32 & ~blk16, m, 0.0), jnp.where(~blk32, m, 0.0)):
        u = off + _bdot(n, off)
        n = n - (u + _bdot(u, n))
    return n


def _gdn_kernel(q_ref, k_ref, v_ref, z_ref, sm_ref, cwq_ref, cwk_ref, cwv_ref, dtb_ref, alog_ref, nw_ref,
                eg_ref, eb_ref,
                o_ref,
                s_ref, qpad_ref, kpad_ref, vpad_ref, qn_ref, kn_ref, vact_ref, gcrep_ref, betarep_ref, xt_ref):
    c = GDN_CHUNK
    rep = GDN_V_HEADS // GDN_QK_HEADS

    @pl.when(pl.program_id(1) == 0)
    def _():
        s_ref[...] = jnp.zeros_like(s_ref)
        qpad_ref[0:SUBLANES, :] = jnp.zeros((SUBLANES, D_QK), F32)
        kpad_ref[0:SUBLANES, :] = jnp.zeros((SUBLANES, D_QK), F32)
        vpad_ref[0:SUBLANES, :] = jnp.zeros((SUBLANES, D_V), F32)

    qact = _conv_silu(q_ref, qpad_ref, cwq_ref, None, c)
    kact = _conv_silu(k_ref, kpad_ref, cwk_ref, None, c)
    vact_ref[...] = _conv_silu(v_ref, vpad_ref, cwv_ref, None, c)
    for h in range(GDN_QK_HEADS):
        sl = slice(h * GDN_DK, (h + 1) * GDN_DK)
        qh = qact[:, sl]
        kh = kact[:, sl]
        qn_ref[:, sl] = qh * lax.rsqrt(jnp.sum(qh * qh, axis=-1, keepdims=True) + NORM_EPS) * (GDN_DK ** -0.5)
        kn_ref[:, sl] = kh * lax.rsqrt(jnp.sum(kh * kh, axis=-1, keepdims=True) + NORM_EPS)

    row = lax.broadcasted_iota(jnp.int32, (c, c), 0)
    col = lax.broadcasted_iota(jnp.int32, (c, c), 1)
    incl = row >= col
    strict = row > col
    tril = jnp.where(incl, 1.0, 0.0).astype(BF16)

    sm = sm_ref[...]
    lane = lax.broadcasted_iota(jnp.int32, (c, LANES), 1)
    beta = jnp.where((lane >= 64) & (lane < 96), jax.nn.sigmoid(sm), 0.0)
    g = jnp.where(lane >= 96, -jnp.exp(alog_ref[...]) * _softplus(sm + dtb_ref[...]), 0.0)
    gc = _dot_split_rhs(tril, g, 3)
    gcrep_ref[...] = _dot_split_lhs(gc, eg_ref[...], 3)
    betarep_ref[...] = _dot_split_lhs(beta, eb_ref[...], 2)
    xt_ref[...] = jnp.concatenate([gc, jnp.zeros_like(gc)], axis=0).T

    def head_body(hq, carry):
        cq = pl.multiple_of(hq * GDN_DK, GDN_DK)
        qh = qn_ref[:, pl.ds(cq, GDN_DK)]
        kh = kn_ref[:, pl.ds(cq, GDN_DK)]
        kb = kh.astype(BF16)
        kkt = _dot_nt(kb, kb)
        qkt = _dot_nt(qh.astype(BF16), kb)
        for e in range(rep):
            h = hq * rep + e
            ch = pl.multiple_of(h * GDN_DK, GDN_DK)
            gcol = gcrep_ref[:, pl.ds(ch, GDN_DK)]
            bcol = betarep_ref[:, pl.ds(ch, GDN_DK)]
            grow = xt_ref[pl.ds(96 + h, 1), :][:, :c]
            gam = jnp.where(incl, jnp.exp(gcol[:, :c] - grow), 0.0)
            m = jnp.where(strict, kkt * gam * bcol[:, :c], 0.0)
            n = _unit_lower_inverse_minus_eye(m, row, col)
            eg = jnp.exp(gcol)
            rhs = jnp.concatenate([vact_ref[:, pl.ds(ch, GDN_DK)] * bcol, kh * (bcol * eg)], axis=1)
            sol = rhs + _bdot(n, rhs)
            u = sol[:, :GDN_DK]
            w = sol[:, GDN_DK:]
            s = s_ref[h]
            sb = s.astype(BF16)
            v_new = u - jnp.dot(w.astype(BF16), sb, preferred_element_type=F32)
            vb = v_new.astype(BF16)
            o = (jnp.dot((qh * eg).astype(BF16), sb, preferred_element_type=F32)
                 + jnp.dot((qkt * gam).astype(BF16), vb, preferred_element_type=F32))
            glast = gcol[c - 1:c, :]
            kdec = kh * jnp.exp(glast - gcol)
            s_ref[h] = s * jnp.exp(glast) + lax.dot_general(
                kdec.astype(BF16), vb, (((0,), (0,)), ((), ())), preferred_element_type=F32)
            ms = jnp.mean(o * o, axis=-1, keepdims=True)
            z = z_ref[:, pl.ds(ch, GDN_DK)].astype(F32)
            o_ref[:, pl.ds(ch, GDN_DK)] = (o * lax.rsqrt(ms + NORM_EPS) * nw_ref[...] * _silu(z)).astype(o_ref.dtype)
        return carry

    lax.fori_loop(0, GDN_QK_HEADS, head_body, 0)


def _gdn(proj, small, conv_w, dt_bias, a_log, norm_w, bsz, tb):
    c = GDN_CHUNK
    nc = tb // c
    t = bsz * tb
    dtb = jnp.pad(dt_bias, (LANES - GDN_V_HEADS, 0)).reshape(1, LANES)
    alog = jnp.pad(a_log, (LANES - GDN_V_HEADS, 0)).reshape(1, LANES)
    r = jnp.arange(LANES)[:, None]
    head_of_col = jnp.arange(D_V)[None, :] // GDN_DK
    eg = (r == 96 + head_of_col).astype(BF16)
    eb = (r == 64 + head_of_col).astype(BF16)
    const = lambda b, i: (0, 0)
    rowblk = lambda off: (lambda b, i: (b * nc + i, off))
    return pl.pallas_call(
        _gdn_kernel,
        grid=(bsz, nc),
        in_specs=[
            pl.BlockSpec((c, D_QK), rowblk(PROJ_Q // D_QK)),
            pl.BlockSpec((c, D_QK), rowblk(PROJ_K // D_QK)),
            pl.BlockSpec((c, D_V), rowblk(PROJ_V // D_V)),
            pl.BlockSpec((c, D_V), rowblk(PROJ_ZG // D_V)),
            pl.BlockSpec((c, LANES), rowblk(0)),
            pl.BlockSpec((GDN_CONV, D_QK), const),
            pl.BlockSpec((GDN_CONV, D_QK), const),
            pl.BlockSpec((GDN_CONV, D_V), const),
            pl.BlockSpec((1, LANES), const),
            pl.BlockSpec((1, LANES), const),
            pl.BlockSpec((1, GDN_DK), const),
            pl.BlockSpec((LANES, D_V), const),
            pl.BlockSpec((LANES, D_V), const),
        ],
        out_specs=pl.BlockSpec((c, D_V), rowblk(0)),
        out_shape=jax.ShapeDtypeStruct((t, D_V), BF16),
        scratch_shapes=[
            pltpu.VMEM((GDN_V_HEADS, GDN_DK, GDN_DK), F32),
            pltpu.VMEM((c + SUBLANES, D_QK), F32),
            pltpu.VMEM((c + SUBLANES, D_QK), F32),
            pltpu.VMEM((c + SUBLANES, D_V), F32),
            pltpu.VMEM((c, D_QK), F32),
            pltpu.VMEM((c, D_QK), F32),
            pltpu.VMEM((c, D_V), F32),
            pltpu.VMEM((c, D_V), F32),
            pltpu.VMEM((c, D_V), F32),
            pltpu.VMEM((LANES, LANES), F32),
        ],
        compiler_params=_cparams(("parallel", "arbitrary"), 48 << 20),
        name="gdn_scan",
    )(proj, proj, proj, proj, small, conv_w[:, :D_QK], conv_w[:, D_QK:2 * D_QK], conv_w[:, 2 * D_QK:],
      dtb, alog, norm_w.reshape(1, GDN_DK), eg, eb)


def _final_norm_kernel(a_ref, b_ref, g_ref, o_ref):
    x = jnp.concatenate([a_ref[N_META:, :], b_ref[...]], axis=0)
    ms = jnp.mean(x * x, axis=-1, keepdims=True)
    o_ref[...] = (x * lax.rsqrt(ms + NORM_EPS) * g_ref[...]).astype(o_ref.dtype)


def _final_norm(h, gain, bsz, seq, tb):
    d = h.shape[1]
    r = 128
    nb = seq // r
    return pl.pallas_call(
        _final_norm_kernel,
        grid=(bsz, nb),
        in_specs=[
            pl.BlockSpec((r, d), lambda b, i: (b * (tb // r) + i, 0)),
            pl.BlockSpec((N_META, d), lambda b, i: ((b * tb + (i + 1) * r) // N_META, 0)),
            pl.BlockSpec((1, d), lambda b, i: (0, 0)),
        ],
        out_specs=pl.BlockSpec((None, r, d), lambda b, i: (b, i, 0)),
        out_shape=jax.ShapeDtypeStruct((bsz, seq, d), F32),
        compiler_params=_cparams(("parallel", "parallel"), 4 * r * d * 4 * 2 + (8 << 20)),
        name="final_norm",
    )(h, h, gain.reshape(1, d))


def _rearranged_in_proj(w_in):
    o = 0
    parts = {}
    for name, width in (("zm", D_SSM), ("xs", D_SSM), ("bc", D_BC), ("dt", SSM_HEADS), ("q", D_QK), ("k", D_QK),
                        ("v", D_V), ("zg", D_V), ("b", GDN_V_HEADS), ("a", GDN_V_HEADS), ("gm", D_MODEL),
                        ("gg", D_MODEL)):
        parts[name] = w_in[:, o:o + width]
        o += width
    big = jnp.concatenate([parts[n] for n in ("zm", "xs", "zg", "v", "gm", "gg", "bc", "q", "k")], axis=1)
    small = jnp.concatenate([parts["dt"], parts["b"], parts["a"]], axis=1)
    return big.astype(BF16), small.astype(BF16)


def _ffn(h, gain, w_gate_up, w_down):
    xn = _rmsnorm(h, gain)
    a = _ffn_up(xn, w_gate_up.astype(BF16))
    return _mm_res(a, w_down.astype(BF16), h)


def kernel(x, meta_tokens, ffn1_norm, ffn1_w_gate_up, ffn1_w_down, mix_norm, w_in, ssm_conv_w, ssm_conv_b,
           ssm_dt_bias, ssm_a_log, ssm_d, ssm_norm, ssm_w_out, gdn_conv_w, gdn_dt_bias, gdn_a_log, gdn_norm,
           gdn_w_out, gate_b, w_o, ffn2_norm, ffn2_w_gate_up, ffn2_w_down, final_norm):
    bsz, seq, d = x.shape
    ltot = seq + N_META
    tb = -(-ltot // SSM_CHUNK) * SSM_CHUNK
    meta = jnp.broadcast_to(meta_tokens[None].astype(x.dtype), (bsz, N_META, d))
    h = jnp.concatenate([meta, x, jnp.zeros((bsz, tb - ltot, d), x.dtype)], axis=1).reshape(bsz * tb, d)
    depth = w_in.shape[0]
    for i in range(depth):
        h = _ffn(h, ffn1_norm[i], ffn1_w_gate_up[i], ffn1_w_down[i])
        xn = _rmsnorm(h, mix_norm[i])
        w_big, w_small = _rearranged_in_proj(w_in[i])
        proj, small = _inproj(xn, w_big, w_small)
        y = _ssd(proj, small, ssm_conv_w[i], ssm_conv_b[i], ssm_dt_bias[i], ssm_a_log[i], ssm_d[i], ssm_norm[i],
                 bsz, tb)
        o = _gdn(proj, small, gdn_conv_w[i], gdn_dt_bias[i], gdn_a_log[i], gdn_norm[i], bsz, tb)
        merged = _branch_merge(y, o, ssm_w_out[i].astype(BF16), gdn_w_out[i].astype(BF16), proj, gate_b[i])
        h = _mm_res(merged, w_o[i].astype(BF16), h)
        h = _ffn(h, ffn2_norm[i], ffn2_w_gate_up[i], ffn2_w_down[i])
    return _final_norm(h, final_norm, bsz, seq, tb)
```

```python
import functools

import jax
import jax.numpy as jnp
from jax import lax
from jax.experimental import pallas as pl
from jax.experimental.pallas import tpu as pltpu

F32 = jnp.float32
BF16 = jnp.bfloat16

D_MODEL = 4096
N_META = 16
NORM_EPS = 1e-6
D_FF = 2 * D_MODEL
SSM_HEADS = 64
SSM_HEAD_DIM = 64
SSM_GROUPS = 8
SSM_STATE = 128
SSM_CONV = 4
SSM_CHUNK = 128
D_SSM = SSM_HEADS * SSM_HEAD_DIM
D_BC = 2 * SSM_GROUPS * SSM_STATE
GDN_DK = 128
GDN_QK_HEADS = 16
GDN_V_HEADS = 32
GDN_CONV = 4
GDN_CHUNK = 64
GDN_HEADS_PER_ITER = 32
D_QK = GDN_QK_HEADS * GDN_DK
D_V = GDN_V_HEADS * GDN_DK

LANES = 128
SUBLANES = 8
VMEM_CAP = 60 * 1024 * 1024

PROJ_ZM, PROJ_XS, PROJ_ZG, PROJ_V, PROJ_GM, PROJ_GG = 0, 4096, 8192, 12288, 16384, 20480
PROJ_BC, PROJ_Q, PROJ_K = 24576, 26624, 28672
N_PROJ = 30720


def _cparams(sem, vmem_bytes):
    return pltpu.CompilerParams(dimension_semantics=sem, vmem_limit_bytes=min(int(vmem_bytes), VMEM_CAP))


def _pick_tile(n, candidates):
    for c in candidates:
        if n % c == 0:
            return c
    raise ValueError(f"no tile for {n} in {candidates}")


def _silu(x):
    return x * jax.nn.sigmoid(x)


def _softplus(x):
    return jnp.maximum(x, 0.0) + jnp.log(1.0 + jnp.exp(-jnp.abs(x)))


def _split_bf16(x, passes):
    parts = []
    r = x
    for p in range(passes):
        b = r.astype(BF16)
        parts.append(b)
        if p + 1 < passes:
            r = r - b.astype(F32)
    return parts


def _dot_split_rhs(a_bf16, x, passes):
    acc = None
    for p in _split_bf16(x, passes):
        d = jnp.dot(a_bf16, p, preferred_element_type=F32)
        acc = d if acc is None else acc + d
    return acc


def _dot_split_lhs(x, e_bf16, passes):
    acc = None
    for p in _split_bf16(x, passes):
        d = jnp.dot(p, e_bf16, preferred_element_type=F32)
        acc = d if acc is None else acc + d
    return acc


def _dot_nt(a, b):
    return lax.dot_general(a, b, (((1,), (1,)), ((), ())), preferred_element_type=F32)


def _bdot(a, b):
    return jnp.dot(a.astype(BF16), b.astype(BF16), preferred_element_type=F32)


def _rmsnorm_kernel(h_ref, g_ref, o_ref):
    x = h_ref[...]
    ms = jnp.mean(x * x, axis=-1, keepdims=True)
    o_ref[...] = (x * lax.rsqrt(ms + NORM_EPS) * g_ref[...]).astype(o_ref.dtype)


def _rmsnorm(h, gain):
    t, d = h.shape
    tm = _pick_tile(t, (256, 128))
    return pl.pallas_call(
        _rmsnorm_kernel,
        grid=(t // tm,),
        in_specs=[pl.BlockSpec((tm, d), lambda i: (i, 0)), pl.BlockSpec((1, d), lambda i: (0, 0))],
        out_specs=pl.BlockSpec((tm, d), lambda i: (i, 0)),
        out_shape=jax.ShapeDtypeStruct((t, d), BF16),
        compiler_params=_cparams(("parallel",), 4 * tm * d * (4 + 2) + (4 << 20)),
        name="rmsnorm",
    )(h, gain.reshape(1, d))


def _ffn_up_kernel(x_ref, wg_ref, wu_ref, o_ref):
    x = x_ref[...]
    g = jnp.dot(x, wg_ref[...], preferred_element_type=F32)
    u = jnp.dot(x, wu_ref[...], preferred_element_type=F32)
    o_ref[...] = (_silu(g) * u * 0.5).astype(o_ref.dtype)


def _ffn_up(xn, w_gate_up):
    t, d = xn.shape
    tm = _pick_tile(t, (1280, 640, 256, 128))
    tn = 512
    nj = D_FF // tn
    vmem = 2 * (tm * d * 2 + 2 * d * tn * 2 + tm * tn * 2) + 4 * tm * tn * 4 + (4 << 20)
    return pl.pallas_call(
        _ffn_up_kernel,
        grid=(t // tm, nj),
        in_specs=[
            pl.BlockSpec((tm, d), lambda i, j: (i, 0)),
            pl.BlockSpec((d, tn), lambda i, j: (0, j)),
            pl.BlockSpec((d, tn), lambda i, j: (0, j + nj)),
        ],
        out_specs=pl.BlockSpec((tm, tn), lambda i, j: (i, j)),
        out_shape=jax.ShapeDtypeStruct((t, D_FF), BF16),
        compiler_params=_cparams(("parallel", "arbitrary"), vmem),
        name="ffn_up",
    )(xn, w_gate_up, w_gate_up)


def _mm_res_kernel(a_ref, w_ref, h_ref, o_ref, *, nk):
    p = jnp.dot(a_ref[...], w_ref[...], preferred_element_type=F32)
    if nk == 1:
        o_ref[...] = h_ref[...] + p
    else:
        k = pl.program_id(2)

        @pl.when(k == 0)
        def _():
            o_ref[...] = h_ref[...] + p

        @pl.when(k > 0)
        def _():
            o_ref[...] += p


def _mm_res(a, w, h):
    t, kdim = a.shape
    n = w.shape[1]
    tm = _pick_tile(t, (1280, 640, 256, 128))
    tn = 512
    tk = 4096
    nk = kdim // tk
    vmem = 2 * (tm * tk * 2 + tk * tn * 2 + 2 * tm * tn * 4) + 2 * tm * tn * 4 + (4 << 20)
    return pl.pallas_call(
        functools.partial(_mm_res_kernel, nk=nk),
        grid=(t // tm, n // tn, nk),
        in_specs=[
            pl.BlockSpec((tm, tk), lambda i, j, k: (i, k)),
            pl.BlockSpec((tk, tn), lambda i, j, k: (k, j)),
            pl.BlockSpec((tm, tn), lambda i, j, k: (i, j)),
        ],
        out_specs=pl.BlockSpec((tm, tn), lambda i, j, k: (i, j)),
        out_shape=jax.ShapeDtypeStruct((t, n), F32),
        compiler_params=_cparams(("parallel", "parallel", "arbitrary"), vmem),
        name="mm_res",
    )(a, w, h)


def _inproj_kernel(x_ref, w_ref, ws_ref, o_ref, os_ref):
    x = x_ref[...]
    o_ref[...] = jnp.dot(x, w_ref[...], preferred_element_type=F32).astype(o_ref.dtype)

    @pl.when(pl.program_id(1) == 0)
    def _():
        os_ref[...] = jnp.dot(x, ws_ref[...], preferred_element_type=F32)


def _inproj(xn, w_big, w_small):
    t, d = xn.shape
    tm = _pick_tile(t, (1280, 640, 256, 128))
    tn = 1024
    vmem = 2 * (tm * d * 2 + d * tn * 2 + tm * tn * 2 + d * LANES * 2 + tm * LANES * 4) + 2 * tm * tn * 4 + (4 << 20)
    return pl.pallas_call(
        _inproj_kernel,
        grid=(t // tm, N_PROJ // tn),
        in_specs=[
            pl.BlockSpec((tm, d), lambda i, j: (i, 0)),
            pl.BlockSpec((d, tn), lambda i, j: (0, j)),
            pl.BlockSpec((d, LANES), lambda i, j: (0, 0)),
        ],
        out_specs=[
            pl.BlockSpec((tm, tn), lambda i, j: (i, j)),
            pl.BlockSpec((tm, LANES), lambda i, j: (i, 0)),
        ],
        out_shape=[jax.ShapeDtypeStruct((t, N_PROJ), BF16), jax.ShapeDtypeStruct((t, LANES), F32)],
        compiler_params=_cparams(("parallel", "arbitrary"), vmem),
        name="inproj",
    )(xn, w_big, w_small)


def _branch_kernel(y_ref, o_ref, wm_ref, wg_ref, gm_ref, gg_ref, bm_ref, bg_ref, out_ref):
    br_m = jnp.dot(y_ref[...], wm_ref[...], preferred_element_type=F32)
    br_g = jnp.dot(o_ref[...], wg_ref[...], preferred_element_type=F32)
    gate_m = jax.nn.sigmoid(gm_ref[...].astype(F32) + bm_ref[...])
    gate_g = jax.nn.sigmoid(gg_ref[...].astype(F32) + bg_ref[...])
    out_ref[...] = (gate_m * br_m + gate_g * br_g).astype(out_ref.dtype)


def _branch_merge(y, o, w_m, w_g, proj, gate_b):
    t, d = y.shape
    tm = _pick_tile(t, (640, 256, 128))
    tn = 512
    jm, jg = PROJ_GM // tn, PROJ_GG // tn
    nj = d // tn
    vmem = 2 * (2 * tm * d * 2 + 2 * d * tn * 2 + 3 * tm * tn * 2) + 6 * tm * tn * 4 + (4 << 20)
    gb = gate_b.reshape(1, 2 * d)
    return pl.pallas_call(
        _branch_kernel,
        grid=(t // tm, nj),
        in_specs=[
            pl.BlockSpec((tm, d), lambda i, j: (i, 0)),
            pl.BlockSpec((tm, d), lambda i, j: (i, 0)),
            pl.BlockSpec((d, tn), lambda i, j: (0, j)),
            pl.BlockSpec((d, tn), lambda i, j: (0, j)),
            pl.BlockSpec((tm, tn), lambda i, j: (i, j + jm)),
            pl.BlockSpec((tm, tn), lambda i, j: (i, j + jg)),
            pl.BlockSpec((1, tn), lambda i, j: (0, j)),
            pl.BlockSpec((1, tn), lambda i, j: (0, j + nj)),
        ],
        out_specs=pl.BlockSpec((tm, tn), lambda i, j: (i, j)),
        out_shape=jax.ShapeDtypeStruct((t, d), BF16),
        compiler_params=_cparams(("parallel", "arbitrary"), vmem),
        name="branch_merge",
    )(y, o, w_m, w_g, proj, proj, gb, gb)


def _conv_silu(x_ref, pad_ref, w_ref, b_ref, rows):
    pad_ref[SUBLANES:SUBLANES + rows, :] = x_ref[...].astype(F32)
    acc = w_ref[3:4, :] * pad_ref[SUBLANES:SUBLANES + rows, :]
    for k in range(3):
        off = SUBLANES - 3 + k
        acc = acc + w_ref[k:k + 1, :] * pad_ref[off:off + rows, :]
    if b_ref is not None:
        acc = acc + b_ref[...]
    pad_ref[0:SUBLANES, :] = pad_ref[rows:rows + SUBLANES, :]
    return _silu(acc)


def _ssd_kernel(xs_ref, bc_ref, z_ref, sm_ref, cwx_ref, cbx_ref, cwbc_ref, cbbc_ref, dtb_ref, alog_ref,
                drep_ref, nw_ref, e64_ref, e128_ref,
                y_ref,
                st_ref, xpad_ref, bcpad_ref, xact_ref, xbf_ref, xw_ref, bcact_ref, larep_ref, exprep_ref,
                lat_ref, dtt_ref, yacc_ref):
    q = SSM_CHUNK
    hg = SSM_HEADS // SSM_GROUPS
    gw = hg * SSM_HEAD_DIM

    @pl.when(pl.program_id(1) == 0)
    def _():
        st_ref[...] = jnp.zeros_like(st_ref)
        xpad_ref[0:SUBLANES, :] = jnp.zeros((SUBLANES, D_SSM), F32)
        bcpad_ref[0:SUBLANES, :] = jnp.zeros((SUBLANES, D_BC), F32)

    xact = _conv_silu(xs_ref, xpad_ref, cwx_ref, cbx_ref, q)
    xact_ref[...] = xact
    xbf_ref[...] = xact.astype(BF16)
    bcact_ref[...] = _conv_silu(bc_ref, bcpad_ref, cwbc_ref, cbbc_ref, q)

    row = lax.broadcasted_iota(jnp.int32, (q, q), 0)
    col = lax.broadcasted_iota(jnp.int32, (q, q), 1)
    causal = row >= col
    tril = jnp.where(causal, 1.0, 0.0).astype(BF16)

    head_lane = col < SSM_HEADS
    dt = jnp.where(head_lane, _softplus(sm_ref[...] + dtb_ref[...]), 0.0)
    a = -jnp.exp(alog_ref[...])
    la = _dot_split_rhs(tril, dt * a, 3)
    la_last = la[q - 1:q, :]
    larep_ref[...] = _dot_split_lhs(la, e128_ref[...], 3)
    exprep_ref[...] = _dot_split_lhs(jnp.exp(la), e64_ref[...], 2)
    to_end = jnp.exp(la_last - la) * dt
    xw_ref[...] = (xact * _dot_split_lhs(to_end, e64_ref[...], 2)).astype(BF16)
    lat_ref[...] = la.T
    dtt_ref[...] = dt.T

    def group_body(g, carry):
        c0 = pl.multiple_of(g * SSM_STATE, SSM_STATE)
        x0 = pl.multiple_of(g * gw, gw)
        bg = bcact_ref[:, pl.ds(c0, SSM_STATE)]
        cg = bcact_ref[:, pl.ds(SSM_GROUPS * SSM_STATE + c0, SSM_STATE)].astype(BF16)
        cb = _dot_nt(cg, bg.astype(BF16))
        st_g = st_ref[:, pl.ds(x0, gw)]
        dec_g = exprep_ref[:, pl.ds(x0, gw)]
        y_inter = jnp.dot(cg, st_g.astype(BF16), preferred_element_type=F32) * dec_g
        st_ref[:, pl.ds(x0, gw)] = st_g * dec_g[q - 1:q, :] + jnp.dot(
            bg.T.astype(BF16), xw_ref[:, pl.ds(x0, gw)], preferred_element_type=F32)
        for pp in range(hg // 2):
            xp0 = pl.multiple_of(x0 + pp * LANES, LANES)
            xpair = xbf_ref[:, pl.ds(xp0, LANES)]
            res = []
            for e in range(2):
                h = g * hg + pp * 2 + e
                seg = larep_ref[:, pl.ds(pl.multiple_of(h * q, q), q)] - lat_ref[pl.ds(h, 1), :]
                dec = jnp.where(causal, jnp.exp(seg), 0.0)
                lmat = (cb * dec * dtt_ref[pl.ds(h, 1), :]).astype(BF16)
                res.append(jnp.dot(lmat, xpair, preferred_element_type=F32))
            y_intra = jnp.where(col < SSM_HEAD_DIM, res[0], res[1])
            yacc_ref[:, pl.ds(xp0, LANES)] = y_intra + y_inter[:, pp * LANES:(pp + 1) * LANES]
        return carry

    lax.fori_loop(0, SSM_GROUPS, group_body, 0)

    z = z_ref[...].astype(F32)
    y = (yacc_ref[...] + drep_ref[...] * xact_ref[...]) * _silu(z)
    for g in range(SSM_GROUPS):
        yg = y[:, g * gw:(g + 1) * gw]
        ms = jnp.mean(yg * yg, axis=-1, keepdims=True)
        y_ref[:, g * gw:(g + 1) * gw] = (yg * lax.rsqrt(ms + NORM_EPS) * nw_ref[:, g * gw:(g + 1) * gw]).astype(y_ref.dtype)


def _ssd(proj, small, conv_w, conv_b, dt_bias, a_log, d_skip, norm_w, bsz, tb):
    q = SSM_CHUNK
    nc = tb // q
    t = bsz * tb
    pad = LANES - SSM_HEADS
    dtb = jnp.pad(dt_bias, (0, pad)).reshape(1, LANES)
    alog = jnp.pad(a_log, (0, pad)).reshape(1, LANES)
    drep = jnp.repeat(d_skip, SSM_HEAD_DIM).reshape(1, D_SSM)
    r = jnp.arange(LANES)[:, None]
    e64 = (jnp.arange(D_SSM)[None, :] // SSM_HEAD_DIM == r).astype(BF16)
    e128 = (jnp.arange(SSM_HEADS * q)[None, :] // q == r).astype(BF16)
    const = lambda b, c: (0, 0)
    rowblk = lambda off: (lambda b, c: (b * nc + c, off))
    return pl.pallas_call(
        _ssd_kernel,
        grid=(bsz, nc),
        in_specs=[
            pl.BlockSpec((q, D_SSM), rowblk(PROJ_XS // D_SSM)),
            pl.BlockSpec((q, D_BC), rowblk(PROJ_BC // D_BC)),
            pl.BlockSpec((q, D_SSM), rowblk(PROJ_ZM // D_SSM)),
            pl.BlockSpec((q, LANES), rowblk(0)),
            pl.BlockSpec((SSM_CONV, D_SSM), const),
            pl.BlockSpec((1, D_SSM), const),
            pl.BlockSpec((SSM_CONV, D_BC), const),
            pl.BlockSpec((1, D_BC), const),
            pl.BlockSpec((1, LANES), const),
            pl.BlockSpec((1, LANES), const),
            pl.BlockSpec((1, D_SSM), const),
            pl.BlockSpec((1, D_SSM), const),
            pl.BlockSpec((LANES, D_SSM), const),
            pl.BlockSpec((LANES, SSM_HEADS * q), const),
        ],
        out_specs=pl.BlockSpec((q, D_SSM), rowblk(0)),
        out_shape=jax.ShapeDtypeStruct((t, D_SSM), BF16),
        scratch_shapes=[
            pltpu.VMEM((SSM_STATE, D_SSM), F32),
            pltpu.VMEM((q + SUBLANES, D_SSM), F32),
            pltpu.VMEM((q + SUBLANES, D_BC), F32),
            pltpu.VMEM((q, D_SSM), F32),
            pltpu.VMEM((q, D_SSM), BF16),
            pltpu.VMEM((q, D_SSM), BF16),
            pltpu.VMEM((q, D_BC), F32),
            pltpu.VMEM((q, SSM_HEADS * q), F32),
            pltpu.VMEM((q, D_SSM), F32),
            pltpu.VMEM((LANES, q), F32),
            pltpu.VMEM((LANES, q), F32),
            pltpu.VMEM((q, D_SSM), F32),
        ],
        compiler_params=_cparams(("parallel", "arbitrary"), 56 << 20),
        name="ssd_scan",
    )(proj, proj, proj, small, conv_w[:, :D_SSM], conv_b[:D_SSM].reshape(1, D_SSM), conv_w[:, D_SSM:],
      conv_b[D_SSM:].reshape(1, D_BC), dtb, alog, drep, norm_w.reshape(1, D_SSM), e64, e128)


def _unit_lower_inverse_minus_eye(ms, row, col):
    blk16 = (row ^ col) < 16
    blk32 = (row ^ col) < 32
    ps = [jnp.where(blk16, m, 0.0) for m in ms]
    ns = [-p for p in ps]
    for _ in range(3):
        ps = [_bdot(p, p) for p in ps]
        ns = [n + p + _bdot(n, p) for n, p in zip(ns, ps)]
    for sel in (blk32 & ~blk16, ~blk32):
        offs = [jnp.where(sel, m, 0.0) for m in ms]
        us = [off + _bdot(n, off) for n, off in zip(ns, offs)]
        ns = [n - (u + _bdot(u, n)) for n, u in zip(ns, us)]
    return ns


def _gdn_kernel(q_ref, k_ref, v_ref, z_ref, sm_ref, cwq_ref, cwk_ref, cwv_ref, dtb_ref, alog_ref, nw_ref,
                eg_ref, eb_ref,
                o_ref,
                s_ref, qpad_ref, kpad_ref, vpad_ref, qn_ref, kn_ref, vact_ref, gcrep_ref, betarep_ref, xt_ref):
    c = GDN_CHUNK
    rep = GDN_V_HEADS // GDN_QK_HEADS

    @pl.when(pl.program_id(1) == 0)
    def _():
        s_ref[...] = jnp.zeros_like(s_ref)
        qpad_ref[0:SUBLANES, :] = jnp.zeros((SUBLANES, D_QK), F32)
        kpad_ref[0:SUBLANES, :] = jnp.zeros((SUBLANES, D_QK), F32)
        vpad_ref[0:SUBLANES, :] = jnp.zeros((SUBLANES, D_V), F32)

    qact = _conv_silu(q_ref, qpad_ref, cwq_ref, None, c)
    kact = _conv_silu(k_ref, kpad_ref, cwk_ref, None, c)
    vact_ref[...] = _conv_silu(v_ref, vpad_ref, cwv_ref, None, c)
    for h in range(GDN_QK_HEADS):
        sl = slice(h * GDN_DK, (h + 1) * GDN_DK)
        qh = qact[:, sl]
        kh = kact[:, sl]
        qn_ref[:, sl] = qh * lax.rsqrt(jnp.sum(qh * qh, axis=-1, keepdims=True) + NORM_EPS) * (GDN_DK ** -0.5)
        kn_ref[:, sl] = kh * lax.rsqrt(jnp.sum(kh * kh, axis=-1, keepdims=True) + NORM_EPS)

    row = lax.broadcasted_iota(jnp.int32, (c, c), 0)
    col = lax.broadcasted_iota(jnp.int32, (c, c), 1)
    incl = row >= col
    strict = row > col
    tril = jnp.where(incl, 1.0, 0.0).astype(BF16)

    sm = sm_ref[...]
    lane = lax.broadcasted_iota(jnp.int32, (c, LANES), 1)
    beta = jnp.where((lane >= 64) & (lane < 96), jax.nn.sigmoid(sm), 0.0)
    g = jnp.where(lane >= 96, -jnp.exp(alog_ref[...]) * _softplus(sm + dtb_ref[...]), 0.0)
    gc = _dot_split_rhs(tril, g, 3)
    gcrep_ref[...] = _dot_split_lhs(gc, eg_ref[...], 3)
    betarep_ref[...] = _dot_split_lhs(beta, eb_ref[...], 2)
    xt_ref[...] = jnp.concatenate([gc, jnp.zeros_like(gc)], axis=0).T

    def heads_body(it, carry):
        nv = GDN_HEADS_PER_ITER
        vheads = [it * nv + e for e in range(nv)]
        chs = [pl.multiple_of(h * GDN_DK, GDN_DK) for h in vheads]
        qhs, khs, kkts, qkts = [], [], [], []
        for e in range(nv // rep):
            cq = pl.multiple_of((it * (nv // rep) + e) * GDN_DK, GDN_DK)
            qh = qn_ref[:, pl.ds(cq, GDN_DK)]
            kh = kn_ref[:, pl.ds(cq, GDN_DK)]
            kb = kh.astype(BF16)
            kkt = _dot_nt(kb, kb)
            qkt = _dot_nt(qh.astype(BF16), kb)
            for _ in range(rep):
                qhs.append(qh)
                khs.append(kh)
                kkts.append(kkt)
                qkts.append(qkt)
        gcols = [gcrep_ref[:, pl.ds(ch, GDN_DK)] for ch in chs]
        bcols = [betarep_ref[:, pl.ds(ch, GDN_DK)] for ch in chs]
        grows = [xt_ref[pl.ds(96 + h, 1), :][:, :c] for h in vheads]
        gams = [jnp.where(incl, jnp.exp(gcol[:, :c] - grow), 0.0) for gcol, grow in zip(gcols, grows)]
        ms = [jnp.where(strict, kkt * gam * bcol[:, :c], 0.0) for kkt, gam, bcol in zip(kkts, gams, bcols)]
        ns = _unit_lower_inverse_minus_eye(ms, row, col)
        egs = [jnp.exp(gcol) for gcol in gcols]
        rhss = [jnp.concatenate([vact_ref[:, pl.ds(ch, GDN_DK)] * bcol, kh * (bcol * eg)], axis=1)
                for ch, bcol, kh, eg in zip(chs, bcols, khs, egs)]
        sols = [rhs + _bdot(n, rhs) for n, rhs in zip(ns, rhss)]
        ss = [s_ref[h] for h in vheads]
        sbs = [s.astype(BF16) for s in ss]
        vbs = [(sol[:, :GDN_DK] - jnp.dot(sol[:, GDN_DK:].astype(BF16), sb, preferred_element_type=F32)).astype(BF16)
               for sol, sb in zip(sols, sbs)]
        os_ = [jnp.dot((qh * eg).astype(BF16), sb, preferred_element_type=F32)
               + jnp.dot((qkt * gam).astype(BF16), vb, preferred_element_type=F32)
               for qh, eg, sb, qkt, gam, vb in zip(qhs, egs, sbs, qkts, gams, vbs)]
        for h, s, gcol, kh, vb in zip(vheads, ss, gcols, khs, vbs):
            glast = gcol[c - 1:c, :]
            kdec = kh * jnp.exp(glast - gcol)
            s_ref[h] = s * jnp.exp(glast) + lax.dot_general(
                kdec.astype(BF16), vb, (((0,), (0,)), ((), ())), preferred_element_type=F32)
        for ch, o in zip(chs, os_):
            msq = jnp.mean(o * o, axis=-1, keepdims=True)
            z = z_ref[:, pl.ds(ch, GDN_DK)].astype(F32)
            o_ref[:, pl.ds(ch, GDN_DK)] = (o * lax.rsqrt(msq + NORM_EPS) * nw_ref[...] * _silu(z)).astype(o_ref.dtype)
        return carry

    lax.fori_loop(0, GDN_V_HEADS // GDN_HEADS_PER_ITER, heads_body, 0)


def _gdn(proj, small, conv_w, dt_bias, a_log, norm_w, bsz, tb):
    c = GDN_CHUNK
    nc = tb // c
    t = bsz * tb
    dtb = jnp.pad(dt_bias, (LANES - GDN_V_HEADS, 0)).reshape(1, LANES)
    alog = jnp.pad(a_log, (LANES - GDN_V_HEADS, 0)).reshape(1, LANES)
    r = jnp.arange(LANES)[:, None]
    head_of_col = jnp.arange(D_V)[None, :] // GDN_DK
    eg = (r == 96 + head_of_col).astype(BF16)
    eb = (r == 64 + head_of_col).astype(BF16)
    const = lambda b, i: (0, 0)
    rowblk = lambda off: (lambda b, i: (b * nc + i, off))
    return pl.pallas_call(
        _gdn_kernel,
        grid=(bsz, nc),
        in_specs=[
            pl.BlockSpec((c, D_QK), rowblk(PROJ_Q // D_QK)),
            pl.BlockSpec((c, D_QK), rowblk(PROJ_K // D_QK)),
            pl.BlockSpec((c, D_V), rowblk(PROJ_V // D_V)),
            pl.BlockSpec((c, D_V), rowblk(PROJ_ZG // D_V)),
            pl.BlockSpec((c, LANES), rowblk(0)),
            pl.BlockSpec((GDN_CONV, D_QK), const),
            pl.BlockSpec((GDN_CONV, D_QK), const),
            pl.BlockSpec((GDN_CONV, D_V), const),
            pl.BlockSpec((1, LANES), const),
            pl.BlockSpec((1, LANES), const),
            pl.BlockSpec((1, GDN_DK), const),
            pl.BlockSpec((LANES, D_V), const),
            pl.BlockSpec((LANES, D_V), const),
        ],
        out_specs=pl.BlockSpec((c, D_V), rowblk(0)),
        out_shape=jax.ShapeDtypeStruct((t, D_V), BF16),
        scratch_shapes=[
            pltpu.VMEM((GDN_V_HEADS, GDN_DK, GDN_DK), F32),
            pltpu.VMEM((c + SUBLANES, D_QK), F32),
            pltpu.VMEM((c + SUBLANES, D_QK), F32),
            pltpu.VMEM((c + SUBLANES, D_V), F32),
            pltpu.VMEM((c, D_QK), F32),
            pltpu.VMEM((c, D_QK), F32),
            pltpu.VMEM((c, D_V), F32),
            pltpu.VMEM((c, D_V), F32),
            pltpu.VMEM((c, D_V), F32),
            pltpu.VMEM((LANES, LANES), F32),
        ],
        compiler_params=_cparams(("parallel", "arbitrary"), 48 << 20),
        name="gdn_scan",
    )(proj, proj, proj, proj, small, conv_w[:, :D_QK], conv_w[:, D_QK:2 * D_QK], conv_w[:, 2 * D_QK:],
      dtb, alog, norm_w.reshape(1, GDN_DK), eg, eb)


def _final_norm_kernel(a_ref, b_ref, g_ref, o_ref):
    x = jnp.concatenate([a_ref[N_META:, :], b_ref[...]], axis=0)
    ms = jnp.mean(x * x, axis=-1, keepdims=True)
    o_ref[...] = (x * lax.rsqrt(ms + NORM_EPS) * g_ref[...]).astype(o_ref.dtype)


def _final_norm(h, gain, bsz, seq, tb):
    d = h.shape[1]
    r = 128
    nb = seq // r
    return pl.pallas_call(
        _final_norm_kernel,
        grid=(bsz, nb),
        in_specs=[
            pl.BlockSpec((r, d), lambda b, i: (b * (tb // r) + i, 0)),
            pl.BlockSpec((N_META, d), lambda b, i: ((b * tb + (i + 1) * r) // N_META, 0)),
            pl.BlockSpec((1, d), lambda b, i: (0, 0)),
        ],
        out_specs=pl.BlockSpec((None, r, d), lambda b, i: (b, i, 0)),
        out_shape=jax.ShapeDtypeStruct((bsz, seq, d), F32),
        compiler_params=_cparams(("parallel", "parallel"), 4 * r * d * 4 * 2 + (8 << 20)),
        name="final_norm",
    )(h, h, gain.reshape(1, d))


def _rearranged_in_proj(w_in):
    o = 0
    parts = {}
    for name, width in (("zm", D_SSM), ("xs", D_SSM), ("bc", D_BC), ("dt", SSM_HEADS), ("q", D_QK), ("k", D_QK),
                        ("v", D_V), ("zg", D_V), ("b", GDN_V_HEADS), ("a", GDN_V_HEADS), ("gm", D_MODEL),
                        ("gg", D_MODEL)):
        parts[name] = w_in[:, o:o + width]
        o += width
    big = jnp.concatenate([parts[n] for n in ("zm", "xs", "zg", "v", "gm", "gg", "bc", "q", "k")], axis=1)
    small = jnp.concatenate([parts["dt"], parts["b"], parts["a"]], axis=1)
    return big.astype(BF16), small.astype(BF16)


def _ffn(h, gain, w_gate_up, w_down):
    xn = _rmsnorm(h, gain)
    a = _ffn_up(xn, w_gate_up.astype(BF16))
    return _mm_res(a, w_down.astype(BF16), h)


def kernel(x, meta_tokens, ffn1_norm, ffn1_w_gate_up, ffn1_w_down, mix_norm, w_in, ssm_conv_w, ssm_conv_b,
           ssm_dt_bias, ssm_a_log, ssm_d, ssm_norm, ssm_w_out, gdn_conv_w, gdn_dt_bias, gdn_a_log, gdn_norm,
           gdn_w_out, gate_b, w_o, ffn2_norm, ffn2_w_gate_up, ffn2_w_down, final_norm):
    bsz, seq, d = x.shape
    ltot = seq + N_META
    tb = -(-ltot // SSM_CHUNK) * SSM_CHUNK
    meta = jnp.broadcast_to(meta_tokens[None].astype(x.dtype), (bsz, N_META, d))
    h = jnp.concatenate([meta, x, jnp.zeros((bsz, tb - ltot, d), x.dtype)], axis=1).reshape(bsz * tb, d)
    depth = w_in.shape[0]
    for i in range(depth):
        h = _ffn(h, ffn1_norm[i], ffn1_w_gate_up[i], ffn1_w_down[i])
        xn = _rmsnorm(h, mix_norm[i])
        w_big, w_small = _rearranged_in_proj(w_in[i])
        proj, small = _inproj(xn, w_big, w_small)
        y = _ssd(proj, small, ssm_conv_w[i], ssm_conv_b[i], ssm_dt_bias[i], ssm_a_log[i], ssm_d[i], ssm_norm[i],
                 bsz, tb)
        o = _gdn(proj, small, gdn_conv_w[i], gdn_dt_bias[i], gdn_a_log[i], gdn_norm[i], bsz, tb)
        merged = _branch_merge(y, o, ssm_w_out[i].astype(BF16), gdn_w_out[i].astype(BF16), proj, gate_b[i])
        h = _mm_res(merged, w_o[i].astype(BF16), h)
        h = _ffn(h, ffn2_norm[i], ffn2_w_gate_up[i], ffn2_w_down[i])
    return _final_norm(h, final_norm, bsz, seq, tb)
```

```python
import functools

import jax
import jax.numpy as jnp
from jax import lax
from jax.experimental import pallas as pl
from jax.experimental.pallas import tpu as pltpu

F32 = jnp.float32
BF16 = jnp.bfloat16

D_MODEL = 4096
N_META = 16
NORM_EPS = 1e-6
D_FF = 2 * D_MODEL
SSM_HEADS = 64
SSM_HEAD_DIM = 64
SSM_GROUPS = 8
SSM_STATE = 128
SSM_CONV = 4
SSM_CHUNK = 128
D_SSM = SSM_HEADS * SSM_HEAD_DIM
D_BC = 2 * SSM_GROUPS * SSM_STATE
GDN_DK = 128
GDN_QK_HEADS = 16
GDN_V_HEADS = 32
GDN_CONV = 4
GDN_CHUNK = 64
GDN_HEADS_PER_ITER = 32
D_QK = GDN_QK_HEADS * GDN_DK
D_V = GDN_V_HEADS * GDN_DK

LANES = 128
SUBLANES = 8
VMEM_CAP = 60 * 1024 * 1024

PROJ_ZM, PROJ_XS, PROJ_ZG, PROJ_V, PROJ_GM, PROJ_GG = 0, 4096, 8192, 12288, 16384, 20480
PROJ_BC, PROJ_Q, PROJ_K = 24576, 26624, 28672
N_PROJ = 30720


def _cparams(sem, vmem_bytes):
    return pltpu.CompilerParams(dimension_semantics=sem, vmem_limit_bytes=min(int(vmem_bytes), VMEM_CAP))


def _pick_tile(n, candidates):
    for c in candidates:
        if n % c == 0:
            return c
    raise ValueError(f"no tile for {n} in {candidates}")


def _silu(x):
    return x * jax.nn.sigmoid(x)


def _softplus(x):
    return jnp.maximum(x, 0.0) + jnp.log(1.0 + jnp.exp(-jnp.abs(x)))


def _split_bf16(x, passes):
    parts = []
    r = x
    for p in range(passes):
        b = r.astype(BF16)
        parts.append(b)
        if p + 1 < passes:
            r = r - b.astype(F32)
    return parts


def _dot_split_rhs(a_bf16, x, passes):
    acc = None
    for p in _split_bf16(x, passes):
        d = jnp.dot(a_bf16, p, preferred_element_type=F32)
        acc = d if acc is None else acc + d
    return acc


def _dot_split_lhs(x, e_bf16, passes):
    acc = None
    for p in _split_bf16(x, passes):
        d = jnp.dot(p, e_bf16, preferred_element_type=F32)
        acc = d if acc is None else acc + d
    return acc


def _dot_nt(a, b):
    return lax.dot_general(a, b, (((1,), (1,)), ((), ())), preferred_element_type=F32)


def _bdot(a, b):
    return jnp.dot(a.astype(BF16), b.astype(BF16), preferred_element_type=F32)


def _lane_partial_sumsq(x):
    sq = x * x
    acc = sq[:, 0:LANES]
    for k in range(1, x.shape[1] // LANES):
        acc = acc + sq[:, k * LANES:(k + 1) * LANES]
    return acc


def _row_rscale(ssq):
    return lax.rsqrt(jnp.sum(ssq, axis=-1, keepdims=True) * (1.0 / D_MODEL) + NORM_EPS)


def _norm_prep_kernel(h_ref, g_ref, hg_ref, ssq_ref):
    x = h_ref[...]
    hg_ref[...] = (x * g_ref[...]).astype(hg_ref.dtype)
    ssq_ref[...] = _lane_partial_sumsq(x)


def _norm_prep(h, gain):
    t, d = h.shape
    tm = _pick_tile(t, (256, 128))
    return pl.pallas_call(
        _norm_prep_kernel,
        grid=(t // tm,),
        in_specs=[pl.BlockSpec((tm, d), lambda i: (i, 0)), pl.BlockSpec((1, d), lambda i: (0, 0))],
        out_specs=[pl.BlockSpec((tm, d), lambda i: (i, 0)), pl.BlockSpec((tm, LANES), lambda i: (i, 0))],
        out_shape=[jax.ShapeDtypeStruct((t, d), BF16), jax.ShapeDtypeStruct((t, LANES), F32)],
        compiler_params=_cparams(("parallel",), 4 * tm * d * (4 + 2) + (4 << 20)),
        name="norm_prep",
    )(h, gain.reshape(1, d))


def _tile0_col(i, j):
    return jnp.where(i == 0, j, 0)


def _ffn_up_math(x, r, wg, wu):
    g = jnp.dot(x, wg, preferred_element_type=F32) * r
    u = jnp.dot(x, wu, preferred_element_type=F32) * r
    return (_silu(g) * u * 0.5).astype(BF16)


def _ffn_up_first_kernel(x_ref, ssq_ref, wg_ref, wu_ref, o_ref, wgb_ref, wub_ref):
    wg = wg_ref[...].astype(BF16)
    wu = wu_ref[...].astype(BF16)
    wgb_ref[...] = wg
    wub_ref[...] = wu
    o_ref[...] = _ffn_up_math(x_ref[...], _row_rscale(ssq_ref[...]), wg, wu)


def _ffn_up_rest_kernel(x_ref, ssq_ref, wg_ref, wu_ref, o0_ref, o_ref):
    i = pl.program_id(0)

    @pl.when(i == 0)
    def _():
        o_ref[...] = o0_ref[...]

    @pl.when(i > 0)
    def _():
        o_ref[...] = _ffn_up_math(x_ref[...], _row_rscale(ssq_ref[...]), wg_ref[...], wu_ref[...])


def _ffn_up(hg, ssq, w_gate_up):
    t, d = hg.shape
    tm = _pick_tile(t, (1280, 640, 256, 128))
    tn1 = 256
    nj1 = D_FF // tn1
    vmem1 = 2 * (tm * d * 2 + tm * LANES * 4 + 2 * d * tn1 * (4 + 2) + tm * tn1 * 2) + 6 * tm * tn1 * 4 + (4 << 20)
    a0, wgb, wub = pl.pallas_call(
        _ffn_up_first_kernel,
        grid=(nj1,),
        in_specs=[
            pl.BlockSpec((tm, d), lambda j: (0, 0)),
            pl.BlockSpec((tm, LANES), lambda j: (0, 0)),
            pl.BlockSpec((d, tn1), lambda j: (0, j)),
            pl.BlockSpec((d, tn1), lambda j: (0, j + nj1)),
        ],
        out_specs=[
            pl.BlockSpec((tm, tn1), lambda j: (0, j)),
            pl.BlockSpec((d, tn1), lambda j: (0, j)),
            pl.BlockSpec((d, tn1), lambda j: (0, j)),
        ],
        out_shape=[jax.ShapeDtypeStruct((tm, D_FF), BF16), jax.ShapeDtypeStruct((d, D_FF), BF16),
                   jax.ShapeDtypeStruct((d, D_FF), BF16)],
        compiler_params=_cparams(("arbitrary",), vmem1),
        name="ffn_up_first",
    )(hg, ssq, w_gate_up, w_gate_up)
    if t == tm:
        return a0
    tn = 512
    vmem = 2 * (tm * d * 2 + tm * LANES * 4 + 2 * d * tn * 2 + 2 * tm * tn * 2) + 6 * tm * tn * 4 + (4 << 20)
    return pl.pallas_call(
        _ffn_up_rest_kernel,
        grid=(t // tm, D_FF // tn),
        in_specs=[
            pl.BlockSpec((tm, d), lambda i, j: (i, 0)),
            pl.BlockSpec((tm, LANES), lambda i, j: (i, 0)),
            pl.BlockSpec((d, tn), lambda i, j: (0, j)),
            pl.BlockSpec((d, tn), lambda i, j: (0, j)),
            pl.BlockSpec((tm, tn), lambda i, j: (0, _tile0_col(i, j))),
        ],
        out_specs=pl.BlockSpec((tm, tn), lambda i, j: (i, j)),
        out_shape=jax.ShapeDtypeStruct((t, D_FF), BF16),
        compiler_params=_cparams(("parallel", "arbitrary"), vmem),
        name="ffn_up",
    )(hg, ssq, wgb, wub, a0)


def _mm_res_store(hn, gn_ref, j, o_ref, hg_ref, ssq_ref):
    o_ref[...] = hn
    if hg_ref is None:
        return
    hg_ref[...] = (hn * gn_ref[...]).astype(hg_ref.dtype)
    part = _lane_partial_sumsq(hn)

    @pl.when(j == 0)
    def _():
        ssq_ref[...] = part

    @pl.when(j > 0)
    def _():
        ssq_ref[...] += part


def _mm_res_first_kernel(a_ref, w_ref, h_ref, gn_ref, o_ref, wb_ref, hg_ref=None, ssq_ref=None):
    w = w_ref[...].astype(BF16)
    wb_ref[...] = w
    hn = h_ref[...] + jnp.dot(a_ref[...], w, preferred_element_type=F32)
    _mm_res_store(hn, gn_ref, pl.program_id(0), o_ref, hg_ref, ssq_ref)


def _mm_res_rest_kernel(a_ref, w_ref, h_ref, gn_ref, o0_ref, hg0_ref, ssq0_ref, o_ref, hg_ref, ssq_ref):
    i = pl.program_id(0)

    @pl.when(i == 0)
    def _():
        o_ref[...] = o0_ref[...]
        if hg_ref is not None:
            hg_ref[...] = hg0_ref[...]
            ssq_ref[...] = ssq0_ref[...]

    @pl.when(i > 0)
    def _():
        hn = h_ref[...] + jnp.dot(a_ref[...], w_ref[...], preferred_element_type=F32)
        _mm_res_store(hn, gn_ref, pl.program_id(1), o_ref, hg_ref, ssq_ref)


def _mm_res_rest_plain_kernel(a_ref, w_ref, h_ref, gn_ref, o0_ref, o_ref):
    _mm_res_rest_kernel(a_ref, w_ref, h_ref, gn_ref, o0_ref, None, None, o_ref, None, None)


def _mm_res(a, w, h, next_gain):
    t, kdim = a.shape
    n = w.shape[1]
    emit = next_gain is not None
    gn = (next_gain if emit else jnp.ones((n,), F32)).reshape(1, n)
    tm = _pick_tile(t, (640, 256, 128))
    tn1 = 256
    blk = lambda shape, imap: pl.BlockSpec(shape, imap)
    vmem1 = 2 * (tm * kdim * 2 + kdim * tn1 * (4 + 2) + tm * tn1 * (4 + 4 + 2) + tm * LANES * 4) + 4 * tm * tn1 * 4 + (4 << 20)
    out_specs1 = [blk((tm, tn1), lambda j: (0, j)), blk((kdim, tn1), lambda j: (0, j))]
    out_shape1 = [jax.ShapeDtypeStruct((tm, n), F32), jax.ShapeDtypeStruct((kdim, n), BF16)]
    if emit:
        out_specs1 += [blk((tm, tn1), lambda j: (0, j)), blk((tm, LANES), lambda j: (0, 0))]
        out_shape1 += [jax.ShapeDtypeStruct((tm, n), BF16), jax.ShapeDtypeStruct((tm, LANES), F32)]
    first = pl.pallas_call(
        _mm_res_first_kernel,
        grid=(n // tn1,),
        in_specs=[blk((tm, kdim), lambda j: (0, 0)), blk((kdim, tn1), lambda j: (0, j)),
                  blk((tm, tn1), lambda j: (0, j)), blk((1, tn1), lambda j: (0, j))],
        out_specs=out_specs1,
        out_shape=out_shape1,
        compiler_params=_cparams(("arbitrary",), vmem1),
        name="mm_res_first",
    )(a, w, h, gn)
    if t == tm:
        return (first[0], first[2], first[3]) if emit else first[0]
    tn = 512 if kdim > 4096 else 1024
    vmem = (2 * (tm * kdim * 2 + kdim * tn * 2 + tm * tn * (4 + 4 + 2 + 4 + 2) + 2 * tm * LANES * 4)
            + 3 * tm * tn * 4 + (4 << 20))
    in_specs = [blk((tm, kdim), lambda i, j: (i, 0)), blk((kdim, tn), lambda i, j: (0, j)),
                blk((tm, tn), lambda i, j: (i, j)), blk((1, tn), lambda i, j: (0, j)),
                blk((tm, tn), lambda i, j: (0, _tile0_col(i, j)))]
    out_specs = [blk((tm, tn), lambda i, j: (i, j))]
    out_shape = [jax.ShapeDtypeStruct((t, n), F32)]
    args = [a, first[1], h, gn, first[0]]
    if emit:
        in_specs += [blk((tm, tn), lambda i, j: (0, _tile0_col(i, j))), blk((tm, LANES), lambda i, j: (0, 0))]
        out_specs += [blk((tm, tn), lambda i, j: (i, j)), blk((tm, LANES), lambda i, j: (i, 0))]
        out_shape += [jax.ShapeDtypeStruct((t, n), BF16), jax.ShapeDtypeStruct((t, LANES), F32)]
        args += [first[2], first[3]]
    rest = pl.pallas_call(
        _mm_res_rest_kernel if emit else _mm_res_rest_plain_kernel,
        grid=(t // tm, n // tn),
        in_specs=in_specs,
        out_specs=out_specs,
        out_shape=out_shape,
        compiler_params=_cparams(("parallel", "arbitrary"), vmem),
        name="mm_res",
    )(*args)
    return tuple(rest) if emit else rest[0]


def _inproj_kernel(x_ref, ssq_ref, w_ref, ws_ref, o_ref, os_ref):
    x = x_ref[...]
    r = _row_rscale(ssq_ref[...])
    o_ref[...] = (jnp.dot(x, w_ref[...], preferred_element_type=F32) * r).astype(o_ref.dtype)

    @pl.when(pl.program_id(1) == 0)
    def _():
        os_ref[...] = jnp.dot(x, ws_ref[...], preferred_element_type=F32) * r


def _inproj(hg, ssq, w_big, w_small):
    t, d = hg.shape
    tm = _pick_tile(t, (1280, 640, 256, 128))
    tn = 1024
    vmem = (2 * (tm * d * 2 + tm * LANES * 4 + d * tn * 2 + tm * tn * 2 + d * LANES * 2 + tm * LANES * 4)
            + 2 * tm * tn * 4 + (4 << 20))
    return pl.pallas_call(
        _inproj_kernel,
        grid=(t // tm, N_PROJ // tn),
        in_specs=[
            pl.BlockSpec((tm, d), lambda i, j: (i, 0)),
            pl.BlockSpec((tm, LANES), lambda i, j: (i, 0)),
            pl.BlockSpec((d, tn), lambda i, j: (0, j)),
            pl.BlockSpec((d, LANES), lambda i, j: (0, 0)),
        ],
        out_specs=[
            pl.BlockSpec((tm, tn), lambda i, j: (i, j)),
            pl.BlockSpec((tm, LANES), lambda i, j: (i, 0)),
        ],
        out_shape=[jax.ShapeDtypeStruct((t, N_PROJ), BF16), jax.ShapeDtypeStruct((t, LANES), F32)],
        compiler_params=_cparams(("parallel", "arbitrary"), vmem),
        name="inproj",
    )(hg, ssq, w_big, w_small)


def _branch_math(y, o, wm, wg, gm, gg, bm, bg):
    br_m = jnp.dot(y, wm, preferred_element_type=F32)
    br_g = jnp.dot(o, wg, preferred_element_type=F32)
    gate_m = jax.nn.sigmoid(gm.astype(F32) + bm)
    gate_g = jax.nn.sigmoid(gg.astype(F32) + bg)
    return (gate_m * br_m + gate_g * br_g).astype(BF16)


def _branch_first_kernel(y_ref, o_ref, wm_ref, wg_ref, gm_ref, gg_ref, bm_ref, bg_ref, out_ref, wmb_ref, wgb_ref):
    wm = wm_ref[...].astype(BF16)
    wg = wg_ref[...].astype(BF16)
    wmb_ref[...] = wm
    wgb_ref[...] = wg
    out_ref[...] = _branch_math(y_ref[...], o_ref[...], wm, wg, gm_ref[...], gg_ref[...], bm_ref[...], bg_ref[...])


def _branch_rest_kernel(y_ref, o_ref, wm_ref, wg_ref, gm_ref, gg_ref, bm_ref, bg_ref, out0_ref, out_ref):
    i = pl.program_id(0)

    @pl.when(i == 0)
    def _():
        out_ref[...] = out0_ref[...]

    @pl.when(i > 0)
    def _():
        out_ref[...] = _branch_math(y_ref[...], o_ref[...], wm_ref[...], wg_ref[...], gm_ref[...], gg_ref[...],
                                    bm_ref[...], bg_ref[...])


def _branch_merge(y, o, w_m, w_g, proj, gate_b):
    t, d = y.shape
    tm = _pick_tile(t, (640, 256, 128))
    gb = gate_b.reshape(1, 2 * d)
    tn1 = 256
    nj1 = d // tn1
    vmem1 = 2 * (2 * tm * d * 2 + 2 * d * tn1 * (4 + 2) + 3 * tm * tn1 * 2) + 6 * tm * tn1 * 4 + (4 << 20)
    out0, wmb, wgb = pl.pallas_call(
        _branch_first_kernel,
        grid=(nj1,),
        in_specs=[
            pl.BlockSpec((tm, d), lambda j: (0, 0)),
            pl.BlockSpec((tm, d), lambda j: (0, 0)),
            pl.BlockSpec((d, tn1), lambda j: (0, j)),
            pl.BlockSpec((d, tn1), lambda j: (0, j)),
            pl.BlockSpec((tm, tn1), lambda j: (0, j + PROJ_GM // tn1)),
            pl.BlockSpec((tm, tn1), lambda j: (0, j + PROJ_GG // tn1)),
            pl.BlockSpec((1, tn1), lambda j: (0, j)),
            pl.BlockSpec((1, tn1), lambda j: (0, j + nj1)),
        ],
        out_specs=[
            pl.BlockSpec((tm, tn1), lambda j: (0, j)),
            pl.BlockSpec((d, tn1), lambda j: (0, j)),
            pl.BlockSpec((d, tn1), lambda j: (0, j)),
        ],
        out_shape=[jax.ShapeDtypeStruct((tm, d), BF16), jax.ShapeDtypeStruct((d, d), BF16),
                   jax.ShapeDtypeStruct((d, d), BF16)],
        compiler_params=_cparams(("arbitrary",), vmem1),
        name="branch_merge_first",
    )(y, o, w_m, w_g, proj, proj, gb, gb)
    if t == tm:
        return out0
    tn = 512
    jm, jg = PROJ_GM // tn, PROJ_GG // tn
    nj = d // tn
    vmem = 2 * (2 * tm * d * 2 + 2 * d * tn * 2 + 4 * tm * tn * 2) + 6 * tm * tn * 4 + (4 << 20)
    return pl.pallas_call(
        _branch_rest_kernel,
        grid=(t // tm, nj),
        in_specs=[
            pl.BlockSpec((tm, d), lambda i, j: (i, 0)),
            pl.BlockSpec((tm, d), lambda i, j: (i, 0)),
            pl.BlockSpec((d, tn), lambda i, j: (0, j)),
            pl.BlockSpec((d, tn), lambda i, j: (0, j)),
            pl.BlockSpec((tm, tn), lambda i, j: (i, j + jm)),
            pl.BlockSpec((tm, tn), lambda i, j: (i, j + jg)),
            pl.BlockSpec((1, tn), lambda i, j: (0, j)),
            pl.BlockSpec((1, tn), lambda i, j: (0, j + nj)),
            pl.BlockSpec((tm, tn), lambda i, j: (0, _tile0_col(i, j))),
        ],
        out_specs=pl.BlockSpec((tm, tn), lambda i, j: (i, j)),
        out_shape=jax.ShapeDtypeStruct((t, d), BF16),
        compiler_params=_cparams(("parallel", "arbitrary"), vmem),
        name="branch_merge",
    )(y, o, wmb, wgb, proj, proj, gb, gb, out0)


def _conv_silu(x_ref, pad_ref, w_ref, b_ref, rows):
    pad_ref[SUBLANES:SUBLANES + rows, :] = x_ref[...].astype(F32)
    acc = w_ref[3:4, :] * pad_ref[SUBLANES:SUBLANES + rows, :]
    for k in range(3):
        off = SUBLANES - 3 + k
        acc = acc + w_ref[k:k + 1, :] * pad_ref[off:off + rows, :]
    if b_ref is not None:
        acc = acc + b_ref[...]
    pad_ref[0:SUBLANES, :] = pad_ref[rows:rows + SUBLANES, :]
    return _silu(acc)


def _ssd_kernel(xs_ref, bc_ref, z_ref, sm_ref, cwx_ref, cbx_ref, cwbc_ref, cbbc_ref, dtb_ref, alog_ref,
                drep_ref, nw_ref, e64_ref, e128_ref,
                y_ref,
                st_ref, xpad_ref, bcpad_ref, xact_ref, xbf_ref, xw_ref, bcact_ref, larep_ref, exprep_ref,
                lat_ref, dtt_ref, yacc_ref):
    q = SSM_CHUNK
    hg = SSM_HEADS // SSM_GROUPS
    gw = hg * SSM_HEAD_DIM

    @pl.when(pl.program_id(1) == 0)
    def _():
        st_ref[...] = jnp.zeros_like(st_ref)
        xpad_ref[0:SUBLANES, :] = jnp.zeros((SUBLANES, D_SSM), F32)
        bcpad_ref[0:SUBLANES, :] = jnp.zeros((SUBLANES, D_BC), F32)

    xact = _conv_silu(xs_ref, xpad_ref, cwx_ref, cbx_ref, q)
    xact_ref[...] = xact
    xbf_ref[...] = xact.astype(BF16)
    bcact_ref[...] = _conv_silu(bc_ref, bcpad_ref, cwbc_ref, cbbc_ref, q)

    row = lax.broadcasted_iota(jnp.int32, (q, q), 0)
    col = lax.broadcasted_iota(jnp.int32, (q, q), 1)
    causal = row >= col
    tril = jnp.where(causal, 1.0, 0.0).astype(BF16)

    head_lane = col < SSM_HEADS
    dt = jnp.where(head_lane, _softplus(sm_ref[...] + dtb_ref[...]), 0.0)
    a = -jnp.exp(alog_ref[...])
    la = _dot_split_rhs(tril, dt * a, 3)
    la_last = la[q - 1:q, :]
    larep_ref[...] = _dot_split_lhs(la, e128_ref[...], 3)
    exprep_ref[...] = _dot_split_lhs(jnp.exp(la), e64_ref[...], 2)
    to_end = jnp.exp(la_last - la) * dt
    xw_ref[...] = (xact * _dot_split_lhs(to_end, e64_ref[...], 2)).astype(BF16)
    lat_ref[...] = la.T
    dtt_ref[...] = dt.T

    def group_body(g, carry):
        c0 = pl.multiple_of(g * SSM_STATE, SSM_STATE)
        x0 = pl.multiple_of(g * gw, gw)
        bg = bcact_ref[:, pl.ds(c0, SSM_STATE)]
        cg = bcact_ref[:, pl.ds(SSM_GROUPS * SSM_STATE + c0, SSM_STATE)].astype(BF16)
        cb = _dot_nt(cg, bg.astype(BF16))
        st_g = st_ref[:, pl.ds(x0, gw)]
        dec_g = exprep_ref[:, pl.ds(x0, gw)]
        y_inter = jnp.dot(cg, st_g.astype(BF16), preferred_element_type=F32) * dec_g
        st_ref[:, pl.ds(x0, gw)] = st_g * dec_g[q - 1:q, :] + jnp.dot(
            bg.T.astype(BF16), xw_ref[:, pl.ds(x0, gw)], preferred_element_type=F32)
        for pp in range(hg // 2):
            xp0 = pl.multiple_of(x0 + pp * LANES, LANES)
            xpair = xbf_ref[:, pl.ds(xp0, LANES)]
            res = []
            for e in range(2):
                h = g * hg + pp * 2 + e
                seg = larep_ref[:, pl.ds(pl.multiple_of(h * q, q), q)] - lat_ref[pl.ds(h, 1), :]
                dec = jnp.where(causal, jnp.exp(seg), 0.0)
                lmat = (cb * dec * dtt_ref[pl.ds(h, 1), :]).astype(BF16)
                res.append(jnp.dot(lmat, xpair, preferred_element_type=F32))
            y_intra = jnp.where(col < SSM_HEAD_DIM, res[0], res[1])
            yacc_ref[:, pl.ds(xp0, LANES)] = y_intra + y_inter[:, pp * LANES:(pp + 1) * LANES]
        return carry

    lax.fori_loop(0, SSM_GROUPS, group_body, 0)

    z = z_ref[...].astype(F32)
    y = (yacc_ref[...] + drep_ref[...] * xact_ref[...]) * _silu(z)
    for g in range(SSM_GROUPS):
        yg = y[:, g * gw:(g + 1) * gw]
        ms = jnp.mean(yg * yg, axis=-1, keepdims=True)
        y_ref[:, g * gw:(g + 1) * gw] = (yg * lax.rsqrt(ms + NORM_EPS) * nw_ref[:, g * gw:(g + 1) * gw]).astype(y_ref.dtype)


def _ssd(proj, small, conv_w, conv_b, dt_bias, a_log, d_skip, norm_w, bsz, tb):
    q = SSM_CHUNK
    nc = tb // q
    t = bsz * tb
    pad = LANES - SSM_HEADS
    dtb = jnp.pad(dt_bias, (0, pad)).reshape(1, LANES)
    alog = jnp.pad(a_log, (0, pad)).reshape(1, LANES)
    drep = jnp.repeat(d_skip, SSM_HEAD_DIM).reshape(1, D_SSM)
    r = jnp.arange(LANES)[:, None]
    e64 = (jnp.arange(D_SSM)[None, :] // SSM_HEAD_DIM == r).astype(BF16)
    e128 = (jnp.arange(SSM_HEADS * q)[None, :] // q == r).astype(BF16)
    const = lambda b, c: (0, 0)
    rowblk = lambda off: (lambda b, c: (b * nc + c, off))
    return pl.pallas_call(
        _ssd_kernel,
        grid=(bsz, nc),
        in_specs=[
            pl.BlockSpec((q, D_SSM), rowblk(PROJ_XS // D_SSM)),
            pl.BlockSpec((q, D_BC), rowblk(PROJ_BC // D_BC)),
            pl.BlockSpec((q, D_SSM), rowblk(PROJ_ZM // D_SSM)),
            pl.BlockSpec((q, LANES), rowblk(0)),
            pl.BlockSpec((SSM_CONV, D_SSM), const),
            pl.BlockSpec((1, D_SSM), const),
            pl.BlockSpec((SSM_CONV, D_BC), const),
            pl.BlockSpec((1, D_BC), const),
            pl.BlockSpec((1, LANES), const),
            pl.BlockSpec((1, LANES), const),
            pl.BlockSpec((1, D_SSM), const),
            pl.BlockSpec((1, D_SSM), const),
            pl.BlockSpec((LANES, D_SSM), const),
            pl.BlockSpec((LANES, SSM_HEADS * q), const),
        ],
        out_specs=pl.BlockSpec((q, D_SSM), rowblk(0)),
        out_shape=jax.ShapeDtypeStruct((t, D_SSM), BF16),
        scratch_shapes=[
            pltpu.VMEM((SSM_STATE, D_SSM), F32),
            pltpu.VMEM((q + SUBLANES, D_SSM), F32),
            pltpu.VMEM((q + SUBLANES, D_BC), F32),
            pltpu.VMEM((q, D_SSM), F32),
            pltpu.VMEM((q, D_SSM), BF16),
            pltpu.VMEM((q, D_SSM), BF16),
            pltpu.VMEM((q, D_BC), F32),
            pltpu.VMEM((q, SSM_HEADS * q), F32),
            pltpu.VMEM((q, D_SSM), F32),
            pltpu.VMEM((LANES, q), F32),
            pltpu.VMEM((LANES, q), F32),
            pltpu.VMEM((q, D_SSM), F32),
        ],
        compiler_params=_cparams(("parallel", "arbitrary"), 56 << 20),
        name="ssd_scan",
    )(proj, proj, proj, small, conv_w[:, :D_SSM], conv_b[:D_SSM].reshape(1, D_SSM), conv_w[:, D_SSM:],
      conv_b[D_SSM:].reshape(1, D_BC), dtb, alog, drep, norm_w.reshape(1, D_SSM), e64, e128)


def _unit_lower_inverse_minus_eye(ms, row, col):
    blk16 = (row ^ col) < 16
    blk32 = (row ^ col) < 32
    ps = [jnp.where(blk16, m, 0.0) for m in ms]
    ns = [-p for p in ps]
    for _ in range(3):
        ps = [_bdot(p, p) for p in ps]
        ns = [n + p + _bdot(n, p) for n, p in zip(ns, ps)]
    for sel in (blk32 & ~blk16, ~blk32):
        offs = [jnp.where(sel, m, 0.0) for m in ms]
        us = [off + _bdot(n, off) for n, off in zip(ns, offs)]
        ns = [n - (u + _bdot(u, n)) for n, u in zip(ns, us)]
    return ns


def _gdn_kernel(q_ref, k_ref, v_ref, z_ref, sm_ref, cwq_ref, cwk_ref, cwv_ref, dtb_ref, alog_ref, nw_ref,
                eg_ref, eb_ref,
                o_ref,
                s_ref, qpad_ref, kpad_ref, vpad_ref, qn_ref, kn_ref, vact_ref, gcrep_ref, betarep_ref, xt_ref):
    c = GDN_CHUNK
    rep = GDN_V_HEADS // GDN_QK_HEADS

    @pl.when(pl.program_id(1) == 0)
    def _():
        s_ref[...] = jnp.zeros_like(s_ref)
        qpad_ref[0:SUBLANES, :] = jnp.zeros((SUBLANES, D_QK), F32)
        kpad_ref[0:SUBLANES, :] = jnp.zeros((SUBLANES, D_QK), F32)
        vpad_ref[0:SUBLANES, :] = jnp.zeros((SUBLANES, D_V), F32)

    qact = _conv_silu(q_ref, qpad_ref, cwq_ref, None, c)
    kact = _conv_silu(k_ref, kpad_ref, cwk_ref, None, c)
    vact_ref[...] = _conv_silu(v_ref, vpad_ref, cwv_ref, None, c)
    for h in range(GDN_QK_HEADS):
        sl = slice(h * GDN_DK, (h + 1) * GDN_DK)
        qh = qact[:, sl]
        kh = kact[:, sl]
        qn_ref[:, sl] = qh * lax.rsqrt(jnp.sum(qh * qh, axis=-1, keepdims=True) + NORM_EPS) * (GDN_DK ** -0.5)
        kn_ref[:, sl] = kh * lax.rsqrt(jnp.sum(kh * kh, axis=-1, keepdims=True) + NORM_EPS)

    row = lax.broadcasted_iota(jnp.int32, (c, c), 0)
    col = lax.broadcasted_iota(jnp.int32, (c, c), 1)
    incl = row >= col
    strict = row > col
    tril = jnp.where(incl, 1.0, 0.0).astype(BF16)

    sm = sm_ref[...]
    lane = lax.broadcasted_iota(jnp.int32, (c, LANES), 1)
    beta = jnp.where((lane >= 64) & (lane < 96), jax.nn.sigmoid(sm), 0.0)
    g = jnp.where(lane >= 96, -jnp.exp(alog_ref[...]) * _softplus(sm + dtb_ref[...]), 0.0)
    gc = _dot_split_rhs(tril, g, 3)
    gcrep_ref[...] = _dot_split_lhs(gc, eg_ref[...], 3)
    betarep_ref[...] = _dot_split_lhs(beta, eb_ref[...], 2)
    xt_ref[...] = jnp.concatenate([gc, jnp.zeros_like(gc)], axis=0).T

    def heads_body(it, carry):
        nv = GDN_HEADS_PER_ITER
        vheads = [it * nv + e for e in range(nv)]
        chs = [pl.multiple_of(h * GDN_DK, GDN_DK) for h in vheads]
        qhs, khs, kkts, qkts = [], [], [], []
        for e in range(nv // rep):
            cq = pl.multiple_of((it * (nv // rep) + e) * GDN_DK, GDN_DK)
            qh = qn_ref[:, pl.ds(cq, GDN_DK)]
            kh = kn_ref[:, pl.ds(cq, GDN_DK)]
            kb = kh.astype(BF16)
            kkt = _dot_nt(kb, kb)
            qkt = _dot_nt(qh.astype(BF16), kb)
            for _ in range(rep):
                qhs.append(qh)
                khs.append(kh)
                kkts.append(kkt)
                qkts.append(qkt)
        gcols = [gcrep_ref[:, pl.ds(ch, GDN_DK)] for ch in chs]
        bcols = [betarep_ref[:, pl.ds(ch, GDN_DK)] for ch in chs]
        grows = [xt_ref[pl.ds(96 + h, 1), :][:, :c] for h in vheads]
        gams = [jnp.where(incl, jnp.exp(gcol[:, :c] - grow), 0.0) for gcol, grow in zip(gcols, grows)]
        ms = [jnp.where(strict, kkt * gam * bcol[:, :c], 0.0) for kkt, gam, bcol in zip(kkts, gams, bcols)]
        ns = _unit_lower_inverse_minus_eye(ms, row, col)
        egs = [jnp.exp(gcol) for gcol in gcols]
        rhss = [jnp.concatenate([vact_ref[:, pl.ds(ch, GDN_DK)] * bcol, kh * (bcol * eg)], axis=1)
                for ch, bcol, kh, eg in zip(chs, bcols, khs, egs)]
        sols = [rhs + _bdot(n, rhs) for n, rhs in zip(ns, rhss)]
        ss = [s_ref[h] for h in vheads]
        sbs = [s.astype(BF16) for s in ss]
        vbs = [(sol[:, :GDN_DK] - jnp.dot(sol[:, GDN_DK:].astype(BF16), sb, preferred_element_type=F32)).astype(BF16)
               for sol, sb in zip(sols, sbs)]
        os_ = [jnp.dot((qh * eg).astype(BF16), sb, preferred_element_type=F32)
               + jnp.dot((qkt * gam).astype(BF16), vb, preferred_element_type=F32)
               for qh, eg, sb, qkt, gam, vb in zip(qhs, egs, sbs, qkts, gams, vbs)]
        for h, s, gcol, kh, vb in zip(vheads, ss, gcols, khs, vbs):
            glast = gcol[c - 1:c, :]
            kdec = kh * jnp.exp(glast - gcol)
            s_ref[h] = s * jnp.exp(glast) + lax.dot_general(
                kdec.astype(BF16), vb, (((0,), (0,)), ((), ())), preferred_element_type=F32)
        for ch, o in zip(chs, os_):
            msq = jnp.mean(o * o, axis=-1, keepdims=True)
            z = z_ref[:, pl.ds(ch, GDN_DK)].astype(F32)
            o_ref[:, pl.ds(ch, GDN_DK)] = (o * lax.rsqrt(msq + NORM_EPS) * nw_ref[...] * _silu(z)).astype(o_ref.dtype)
        return carry

    lax.fori_loop(0, GDN_V_HEADS // GDN_HEADS_PER_ITER, heads_body, 0)


def _gdn(proj, small, conv_w, dt_bias, a_log, norm_w, bsz, tb):
    c = GDN_CHUNK
    nc = tb // c
    t = bsz * tb
    dtb = jnp.pad(dt_bias, (LANES - GDN_V_HEADS, 0)).reshape(1, LANES)
    alog = jnp.pad(a_log, (LANES - GDN_V_HEADS, 0)).reshape(1, LANES)
    r = jnp.arange(LANES)[:, None]
    head_of_col = jnp.arange(D_V)[None, :] // GDN_DK
    eg = (r == 96 + head_of_col).astype(BF16)
    eb = (r == 64 + head_of_col).astype(BF16)
    const = lambda b, i: (0, 0)
    rowblk = lambda off: (lambda b, i: (b * nc + i, off))
    return pl.pallas_call(
        _gdn_kernel,
        grid=(bsz, nc),
        in_specs=[
            pl.BlockSpec((c, D_QK), rowblk(PROJ_Q // D_QK)),
            pl.BlockSpec((c, D_QK), rowblk(PROJ_K // D_QK)),
            pl.BlockSpec((c, D_V), rowblk(PROJ_V // D_V)),
            pl.BlockSpec((c, D_V), rowblk(PROJ_ZG // D_V)),
            pl.BlockSpec((c, LANES), rowblk(0)),
            pl.BlockSpec((GDN_CONV, D_QK), const),
            pl.BlockSpec((GDN_CONV, D_QK), const),
            pl.BlockSpec((GDN_CONV, D_V), const),
            pl.BlockSpec((1, LANES), const),
            pl.BlockSpec((1, LANES), const),
            pl.BlockSpec((1, GDN_DK), const),
            pl.BlockSpec((LANES, D_V), const),
            pl.BlockSpec((LANES, D_V), const),
        ],
        out_specs=pl.BlockSpec((c, D_V), rowblk(0)),
        out_shape=jax.ShapeDtypeStruct((t, D_V), BF16),
        scratch_shapes=[
            pltpu.VMEM((GDN_V_HEADS, GDN_DK, GDN_DK), F32),
            pltpu.VMEM((c + SUBLANES, D_QK), F32),
            pltpu.VMEM((c + SUBLANES, D_QK), F32),
            pltpu.VMEM((c + SUBLANES, D_V), F32),
            pltpu.VMEM((c, D_QK), F32),
            pltpu.VMEM((c, D_QK), F32),
            pltpu.VMEM((c, D_V), F32),
            pltpu.VMEM((c, D_V), F32),
            pltpu.VMEM((c, D_V), F32),
            pltpu.VMEM((LANES, LANES), F32),
        ],
        compiler_params=_cparams(("parallel", "arbitrary"), 48 << 20),
        name="gdn_scan",
    )(proj, proj, proj, proj, small, conv_w[:, :D_QK], conv_w[:, D_QK:2 * D_QK], conv_w[:, 2 * D_QK:],
      dtb, alog, norm_w.reshape(1, GDN_DK), eg, eb)


def _final_norm_kernel(a_ref, b_ref, g_ref, o_ref):
    x = jnp.concatenate([a_ref[N_META:, :], b_ref[...]], axis=0)
    ms = jnp.mean(x * x, axis=-1, keepdims=True)
    o_ref[...] = (x * lax.rsqrt(ms + NORM_EPS) * g_ref[...]).astype(o_ref.dtype)


def _final_norm(h, gain, bsz, seq, tb):
    d = h.shape[1]
    r = 128
    nb = seq // r
    return pl.pallas_call(
        _final_norm_kernel,
        grid=(bsz, nb),
        in_specs=[
            pl.BlockSpec((r, d), lambda b, i: (b * (tb // r) + i, 0)),
            pl.BlockSpec((N_META, d), lambda b, i: ((b * tb + (i + 1) * r) // N_META, 0)),
            pl.BlockSpec((1, d), lambda b, i: (0, 0)),
        ],
        out_specs=pl.BlockSpec((None, r, d), lambda b, i: (b, i, 0)),
        out_shape=jax.ShapeDtypeStruct((bsz, seq, d), F32),
        compiler_params=_cparams(("parallel", "parallel"), 4 * r * d * 4 * 2 + (8 << 20)),
        name="final_norm",
    )(h, h, gain.reshape(1, d))


def _rearranged_in_proj(w_in):
    o = 0
    parts = {}
    for name, width in (("zm", D_SSM), ("xs", D_SSM), ("bc", D_BC), ("dt", SSM_HEADS), ("q", D_QK), ("k", D_QK),
                        ("v", D_V), ("zg", D_V), ("b", GDN_V_HEADS), ("a", GDN_V_HEADS), ("gm", D_MODEL),
                        ("gg", D_MODEL)):
        parts[name] = w_in[:, o:o + width]
        o += width
    big = jnp.concatenate([parts[n] for n in ("zm", "xs", "zg", "v", "gm", "gg", "bc", "q", "k")], axis=1)
    small = jnp.concatenate([parts["dt"], parts["b"], parts["a"]], axis=1)
    return big.astype(BF16), small.astype(BF16)


def kernel(x, meta_tokens, ffn1_norm, ffn1_w_gate_up, ffn1_w_down, mix_norm, w_in, ssm_conv_w, ssm_conv_b,
           ssm_dt_bias, ssm_a_log, ssm_d, ssm_norm, ssm_w_out, gdn_conv_w, gdn_dt_bias, gdn_a_log, gdn_norm,
           gdn_w_out, gate_b, w_o, ffn2_norm, ffn2_w_gate_up, ffn2_w_down, final_norm):
    bsz, seq, d = x.shape
    ltot = seq + N_META
    tb = -(-ltot // SSM_CHUNK) * SSM_CHUNK
    meta = jnp.broadcast_to(meta_tokens[None].astype(x.dtype), (bsz, N_META, d))
    h = jnp.concatenate([meta, x, jnp.zeros((bsz, tb - ltot, d), x.dtype)], axis=1).reshape(bsz * tb, d)
    depth = w_in.shape[0]
    hg, ssq = _norm_prep(h, ffn1_norm[0])
    for i in range(depth):
        a = _ffn_up(hg, ssq, ffn1_w_gate_up[i])
        h, hg, ssq = _mm_res(a, ffn1_w_down[i], h, mix_norm[i])
        w_big, w_small = _rearranged_in_proj(w_in[i])
        proj, small = _inproj(hg, ssq, w_big, w_small)
        y = _ssd(proj, small, ssm_conv_w[i], ssm_conv_b[i], ssm_dt_bias[i], ssm_a_log[i], ssm_d[i], ssm_norm[i],
                 bsz, tb)
        o = _gdn(proj, small, gdn_conv_w[i], gdn_dt_bias[i], gdn_a_log[i], gdn_norm[i], bsz, tb)
        merged = _branch_merge(y, o, ssm_w_out[i], gdn_w_out[i], proj, gate_b[i])
        h, hg, ssq = _mm_res(merged, w_o[i], h, ffn2_norm[i])
        a = _ffn_up(hg, ssq, ffn2_w_gate_up[i])
        if i + 1 < depth:
            h, hg, ssq = _mm_res(a, ffn2_w_down[i], h, ffn1_norm[i + 1])
        else:
            h = _mm_res(a, ffn2_w_down[i], h, None)
    return _final_norm(h, final_norm, bsz, seq, tb)
```

```python
import functools

import jax
import jax.numpy as jnp
from jax import lax
from jax.experimental import pallas as pl
from jax.experimental.pallas import tpu as pltpu

F32 = jnp.float32
BF16 = jnp.bfloat16

D_MODEL = 4096
N_META = 16
NORM_EPS = 1e-6
D_FF = 2 * D_MODEL
SSM_HEADS = 64
SSM_HEAD_DIM = 64
SSM_GROUPS = 8
SSM_STATE = 128
SSM_CONV = 4
SSM_CHUNK = 128
D_SSM = SSM_HEADS * SSM_HEAD_DIM
D_BC = 2 * SSM_GROUPS * SSM_STATE
GDN_DK = 128
GDN_QK_HEADS = 16
GDN_V_HEADS = 32
GDN_CONV = 4
GDN_CHUNK = 64
GDN_HEADS_PER_ITER = 32
D_QK = GDN_QK_HEADS * GDN_DK
D_V = GDN_V_HEADS * GDN_DK

LANES = 128
SUBLANES = 8
VMEM_CAP = 60 * 1024 * 1024

PROJ_ZM, PROJ_XS, PROJ_ZG, PROJ_V, PROJ_GM, PROJ_GG = 0, 4096, 8192, 12288, 16384, 20480
PROJ_BC, PROJ_Q, PROJ_K = 24576, 26624, 28672
N_PROJ = 30720


def _cparams(sem, vmem_bytes):
    return pltpu.CompilerParams(dimension_semantics=sem, vmem_limit_bytes=min(int(vmem_bytes), VMEM_CAP))


def _pick_tile(n, candidates):
    for c in candidates:
        if n % c == 0:
            return c
    raise ValueError(f"no tile for {n} in {candidates}")


def _silu(x):
    return x * jax.nn.sigmoid(x)


def _softplus(x):
    return jnp.maximum(x, 0.0) + jnp.log(1.0 + jnp.exp(-jnp.abs(x)))


def _split_bf16(x, passes):
    parts = []
    r = x
    for p in range(passes):
        b = r.astype(BF16)
        parts.append(b)
        if p + 1 < passes:
            r = r - b.astype(F32)
    return parts


def _dot_split_rhs(a_bf16, x, passes):
    acc = None
    for p in _split_bf16(x, passes):
        d = jnp.dot(a_bf16, p, preferred_element_type=F32)
        acc = d if acc is None else acc + d
    return acc


def _two_term(x):
    hi = x.astype(BF16)
    return hi, (x - hi.astype(F32)).astype(BF16)


def _replicate_dot(x, e2_bf16):
    hi, lo = _two_term(x)
    return jnp.dot(jnp.concatenate([hi, lo], axis=1), e2_bf16, preferred_element_type=F32)


def _dot_nt(a, b):
    return lax.dot_general(a, b, (((1,), (1,)), ((), ())), preferred_element_type=F32)


def _bdot(a, b):
    return jnp.dot(a.astype(BF16), b.astype(BF16), preferred_element_type=F32)


def _lane_partial_sumsq(x):
    sq = x * x
    acc = sq[:, 0:LANES]
    for k in range(1, x.shape[1] // LANES):
        acc = acc + sq[:, k * LANES:(k + 1) * LANES]
    return acc


def _row_rscale(ssq):
    return lax.rsqrt(jnp.sum(ssq, axis=-1, keepdims=True) * (1.0 / D_MODEL) + NORM_EPS)


def _embed_prep_kernel(xa_ref, xb_ref, meta_ref, g_ref, h_ref, hg_ref, ssq_ref, *, nxb):
    r = pl.program_id(1)
    rows = h_ref.shape[0]
    top = jnp.where(r == 0, meta_ref[...], jnp.where(r <= nxb, xb_ref[...], 0.0))
    bot = jnp.where(r < nxb, xa_ref[0:rows - N_META, :], 0.0)
    hb = jnp.concatenate([top, bot], axis=0)
    h_ref[...] = hb
    hg_ref[...] = (hb * g_ref[...]).astype(hg_ref.dtype)
    ssq_ref[...] = _lane_partial_sumsq(hb)


def _embed_prep(x, meta_tokens, gain, tb):
    bsz, seq, d = x.shape
    rows = SSM_CHUNK
    nxb = seq // rows
    nrb = tb // rows
    per16 = rows // N_META
    t = bsz * tb
    return pl.pallas_call(
        functools.partial(_embed_prep_kernel, nxb=nxb),
        grid=(bsz, nrb),
        in_specs=[
            pl.BlockSpec((None, rows, d), lambda b, r: (b, jnp.minimum(r, nxb - 1), 0)),
            pl.BlockSpec((None, N_META, d), lambda b, r: (b, jnp.clip(r * per16 - 1, 0, seq // N_META - 1), 0)),
            pl.BlockSpec((N_META, d), lambda b, r: (0, 0)),
            pl.BlockSpec((1, d), lambda b, r: (0, 0)),
        ],
        out_specs=[
            pl.BlockSpec((rows, d), lambda b, r: (b * nrb + r, 0)),
            pl.BlockSpec((rows, d), lambda b, r: (b * nrb + r, 0)),
            pl.BlockSpec((rows, LANES), lambda b, r: (b * nrb + r, 0)),
        ],
        out_shape=[jax.ShapeDtypeStruct((t, d), F32), jax.ShapeDtypeStruct((t, d), BF16),
                   jax.ShapeDtypeStruct((t, LANES), F32)],
        compiler_params=_cparams(("parallel", "parallel"), 4 * rows * d * (4 + 4 + 4 + 2) + (8 << 20)),
        name="embed_prep",
    )(x, x, meta_tokens.astype(x.dtype), gain.reshape(1, d))


def _tile0_col(i, j):
    return jnp.where(i == 0, j, 0)


def _ffn_up_math(x, r, wg, wu):
    g = jnp.dot(x, wg, preferred_element_type=F32) * r
    u = jnp.dot(x, wu, preferred_element_type=F32) * r
    return (_silu(g) * u * 0.5).astype(BF16)


def _ffn_up_first_kernel(x_ref, ssq_ref, wg_ref, wu_ref, o_ref, wgb_ref, wub_ref):
    wg = wg_ref[...].astype(BF16)
    wu = wu_ref[...].astype(BF16)
    wgb_ref[...] = wg
    wub_ref[...] = wu
    o_ref[...] = _ffn_up_math(x_ref[...], _row_rscale(ssq_ref[...]), wg, wu)


def _ffn_up_rest_kernel(x_ref, ssq_ref, wg_ref, wu_ref, o0_ref, o_ref):
    i = pl.program_id(0)

    @pl.when(i == 0)
    def _():
        o_ref[...] = o0_ref[...]

    @pl.when(i > 0)
    def _():
        o_ref[...] = _ffn_up_math(x_ref[...], _row_rscale(ssq_ref[...]), wg_ref[...], wu_ref[...])


def _ffn_up(hg, ssq, w_gate_up, layer):
    t, d = hg.shape
    tm = _pick_tile(t, (1280, 640, 256, 128))
    tn1 = 256
    nj1 = D_FF // tn1
    vmem1 = 2 * (tm * d * 2 + tm * LANES * 4 + 2 * d * tn1 * (4 + 2) + tm * tn1 * 2) + 6 * tm * tn1 * 4 + (4 << 20)
    a0, wgb, wub = pl.pallas_call(
        _ffn_up_first_kernel,
        grid=(nj1,),
        in_specs=[
            pl.BlockSpec((tm, d), lambda j: (0, 0)),
            pl.BlockSpec((tm, LANES), lambda j: (0, 0)),
            pl.BlockSpec((None, d, tn1), lambda j: (layer, 0, j)),
            pl.BlockSpec((None, d, tn1), lambda j: (layer, 0, j + nj1)),
        ],
        out_specs=[
            pl.BlockSpec((tm, tn1), lambda j: (0, j)),
            pl.BlockSpec((d, tn1), lambda j: (0, j)),
            pl.BlockSpec((d, tn1), lambda j: (0, j)),
        ],
        out_shape=[jax.ShapeDtypeStruct((tm, D_FF), BF16), jax.ShapeDtypeStruct((d, D_FF), BF16),
                   jax.ShapeDtypeStruct((d, D_FF), BF16)],
        compiler_params=_cparams(("arbitrary",), vmem1),
        name="ffn_up_first",
    )(hg, ssq, w_gate_up, w_gate_up)
    if t == tm:
        return a0
    tn = 512
    vmem = 2 * (tm * d * 2 + tm * LANES * 4 + 2 * d * tn * 2 + 2 * tm * tn * 2) + 6 * tm * tn * 4 + (4 << 20)
    return pl.pallas_call(
        _ffn_up_rest_kernel,
        grid=(t // tm, D_FF // tn),
        in_specs=[
            pl.BlockSpec((tm, d), lambda i, j: (i, 0)),
            pl.BlockSpec((tm, LANES), lambda i, j: (i, 0)),
            pl.BlockSpec((d, tn), lambda i, j: (0, j)),
            pl.BlockSpec((d, tn), lambda i, j: (0, j)),
            pl.BlockSpec((tm, tn), lambda i, j: (0, _tile0_col(i, j))),
        ],
        out_specs=pl.BlockSpec((tm, tn), lambda i, j: (i, j)),
        out_shape=jax.ShapeDtypeStruct((t, D_FF), BF16),
        compiler_params=_cparams(("parallel", "arbitrary"), vmem),
        name="ffn_up",
    )(hg, ssq, wgb, wub, a0)


def _mm_res_store(hn, gn_ref, j, o_ref, hg_ref, ssq_ref):
    o_ref[...] = hn
    if hg_ref is None:
        return
    hg_ref[...] = (hn * gn_ref[...]).astype(hg_ref.dtype)
    part = _lane_partial_sumsq(hn)

    @pl.when(j == 0)
    def _():
        ssq_ref[...] = part

    @pl.when(j > 0)
    def _():
        ssq_ref[...] += part


def _mm_res_first_kernel(a_ref, w_ref, h_ref, gn_ref, o_ref, wb_ref, hg_ref=None, ssq_ref=None):
    w = w_ref[...].astype(BF16)
    wb_ref[...] = w
    hn = h_ref[...] + jnp.dot(a_ref[...], w, preferred_element_type=F32)
    _mm_res_store(hn, gn_ref, pl.program_id(0), o_ref, hg_ref, ssq_ref)


def _mm_res_rest_kernel(a_ref, w_ref, h_ref, gn_ref, o0_ref, hg0_ref, ssq0_ref, o_ref, hg_ref, ssq_ref):
    i = pl.program_id(0)

    @pl.when(i == 0)
    def _():
        o_ref[...] = o0_ref[...]
        if hg_ref is not None:
            hg_ref[...] = hg0_ref[...]
            ssq_ref[...] = ssq0_ref[...]

    @pl.when(i > 0)
    def _():
        hn = h_ref[...] + jnp.dot(a_ref[...], w_ref[...], preferred_element_type=F32)
        _mm_res_store(hn, gn_ref, pl.program_id(1), o_ref, hg_ref, ssq_ref)


def _mm_res_rest_plain_kernel(a_ref, w_ref, h_ref, gn_ref, o0_ref, o_ref):
    _mm_res_rest_kernel(a_ref, w_ref, h_ref, gn_ref, o0_ref, None, None, o_ref, None, None)


def _mm_res(a, w, layer, h, next_gain):
    t, kdim = a.shape
    n = w.shape[2]
    emit = next_gain is not None
    gn = (next_gain if emit else jnp.ones((n,), F32)).reshape(1, n)
    tm = _pick_tile(t, (640, 256, 128))
    tn1 = 256
    blk = lambda shape, imap: pl.BlockSpec(shape, imap)
    vmem1 = 2 * (tm * kdim * 2 + kdim * tn1 * (4 + 2) + tm * tn1 * (4 + 4 + 2) + tm * LANES * 4) + 4 * tm * tn1 * 4 + (4 << 20)
    out_specs1 = [blk((tm, tn1), lambda j: (0, j)), blk((kdim, tn1), lambda j: (0, j))]
    out_shape1 = [jax.ShapeDtypeStruct((tm, n), F32), jax.ShapeDtypeStruct((kdim, n), BF16)]
    if emit:
        out_specs1 += [blk((tm, tn1), lambda j: (0, j)), blk((tm, LANES), lambda j: (0, 0))]
        out_shape1 += [jax.ShapeDtypeStruct((tm, n), BF16), jax.ShapeDtypeStruct((tm, LANES), F32)]
    first = pl.pallas_call(
        _mm_res_first_kernel,
        grid=(n // tn1,),
        in_specs=[blk((tm, kdim), lambda j: (0, 0)), blk((None, kdim, tn1), lambda j: (layer, 0, j)),
                  blk((tm, tn1), lambda j: (0, j)), blk((1, tn1), lambda j: (0, j))],
        out_specs=out_specs1,
        out_shape=out_shape1,
        compiler_params=_cparams(("arbitrary",), vmem1),
        name="mm_res_first",
    )(a, w, h, gn)
    if t == tm:
        return (first[0], first[2], first[3]) if emit else first[0]
    tn = 512 if kdim > 4096 else 1024
    vmem = (2 * (tm * kdim * 2 + kdim * tn * 2 + tm * tn * (4 + 4 + 2 + 4 + 2) + 2 * tm * LANES * 4)
            + 3 * tm * tn * 4 + (4 << 20))
    in_specs = [blk((tm, kdim), lambda i, j: (i, 0)), blk((kdim, tn), lambda i, j: (0, j)),
                blk((tm, tn), lambda i, j: (i, j)), blk((1, tn), lambda i, j: (0, j)),
                blk((tm, tn), lambda i, j: (0, _tile0_col(i, j)))]
    out_specs = [blk((tm, tn), lambda i, j: (i, j))]
    out_shape = [jax.ShapeDtypeStruct((t, n), F32)]
    args = [a, first[1], h, gn, first[0]]
    if emit:
        in_specs += [blk((tm, tn), lambda i, j: (0, _tile0_col(i, j))), blk((tm, LANES), lambda i, j: (0, 0))]
        out_specs += [blk((tm, tn), lambda i, j: (i, j)), blk((tm, LANES), lambda i, j: (i, 0))]
        out_shape += [jax.ShapeDtypeStruct((t, n), BF16), jax.ShapeDtypeStruct((t, LANES), F32)]
        args += [first[2], first[3]]
    rest = pl.pallas_call(
        _mm_res_rest_kernel if emit else _mm_res_rest_plain_kernel,
        grid=(t // tm, n // tn),
        in_specs=in_specs,
        out_specs=out_specs,
        out_shape=out_shape,
        compiler_params=_cparams(("parallel", "arbitrary"), vmem),
        name="mm_res",
    )(*args)
    return tuple(rest) if emit else rest[0]


def _inproj_kernel(x_ref, ssq_ref, w_ref, ws_ref, o_ref, os_ref):
    x = x_ref[...]
    r = _row_rscale(ssq_ref[...])
    o_ref[...] = (jnp.dot(x, w_ref[...], preferred_element_type=F32) * r).astype(o_ref.dtype)

    @pl.when(pl.program_id(1) == 0)
    def _():
        os_ref[...] = jnp.dot(x, ws_ref[...], preferred_element_type=F32) * r


def _inproj(hg, ssq, w_big, w_small):
    t, d = hg.shape
    tm = _pick_tile(t, (1280, 640, 256, 128))
    tn = 1024
    vmem = (2 * (tm * d * 2 + tm * LANES * 4 + d * tn * 2 + tm * tn * 2 + d * LANES * 2 + tm * LANES * 4)
            + 2 * tm * tn * 4 + (4 << 20))
    return pl.pallas_call(
        _inproj_kernel,
        grid=(t // tm, N_PROJ // tn),
        in_specs=[
            pl.BlockSpec((tm, d), lambda i, j: (i, 0)),
            pl.BlockSpec((tm, LANES), lambda i, j: (i, 0)),
            pl.BlockSpec((d, tn), lambda i, j: (0, j)),
            pl.BlockSpec((d, LANES), lambda i, j: (0, 0)),
        ],
        out_specs=[
            pl.BlockSpec((tm, tn), lambda i, j: (i, j)),
            pl.BlockSpec((tm, LANES), lambda i, j: (i, 0)),
        ],
        out_shape=[jax.ShapeDtypeStruct((t, N_PROJ), BF16), jax.ShapeDtypeStruct((t, LANES), F32)],
        compiler_params=_cparams(("parallel", "arbitrary"), vmem),
        name="inproj",
    )(hg, ssq, w_big, w_small)


def _branch_math(y, o, wm, wg, gm, gg, bm, bg):
    br_m = jnp.dot(y, wm, preferred_element_type=F32)
    br_g = jnp.dot(o, wg, preferred_element_type=F32)
    gate_m = jax.nn.sigmoid(gm.astype(F32) + bm)
    gate_g = jax.nn.sigmoid(gg.astype(F32) + bg)
    return (gate_m * br_m + gate_g * br_g).astype(BF16)


def _branch_first_kernel(y_ref, o_ref, wm_ref, wg_ref, gm_ref, gg_ref, bm_ref, bg_ref, out_ref, wmb_ref, wgb_ref):
    wm = wm_ref[...].astype(BF16)
    wg = wg_ref[...].astype(BF16)
    wmb_ref[...] = wm
    wgb_ref[...] = wg
    out_ref[...] = _branch_math(y_ref[...], o_ref[...], wm, wg, gm_ref[...], gg_ref[...], bm_ref[...], bg_ref[...])


def _branch_rest_kernel(y_ref, o_ref, wm_ref, wg_ref, gm_ref, gg_ref, bm_ref, bg_ref, out0_ref, out_ref):
    i = pl.program_id(0)

    @pl.when(i == 0)
    def _():
        out_ref[...] = out0_ref[...]

    @pl.when(i > 0)
    def _():
        out_ref[...] = _branch_math(y_ref[...], o_ref[...], wm_ref[...], wg_ref[...], gm_ref[...], gg_ref[...],
                                    bm_ref[...], bg_ref[...])


def _branch_merge(y, o, w_m, w_g, layer, proj, gate_b):
    t, d = y.shape
    tm = _pick_tile(t, (640, 256, 128))
    gb = gate_b.reshape(1, 2 * d)
    tn1 = 256
    nj1 = d // tn1
    vmem1 = 2 * (2 * tm * d * 2 + 2 * d * tn1 * (4 + 2) + 3 * tm * tn1 * 2) + 6 * tm * tn1 * 4 + (4 << 20)
    out0, wmb, wgb = pl.pallas_call(
        _branch_first_kernel,
        grid=(nj1,),
        in_specs=[
            pl.BlockSpec((tm, d), lambda j: (0, 0)),
            pl.BlockSpec((tm, d), lambda j: (0, 0)),
            pl.BlockSpec((None, d, tn1), lambda j: (layer, 0, j)),
            pl.BlockSpec((None, d, tn1), lambda j: (layer, 0, j)),
            pl.BlockSpec((tm, tn1), lambda j: (0, j + PROJ_GM // tn1)),
            pl.BlockSpec((tm, tn1), lambda j: (0, j + PROJ_GG // tn1)),
            pl.BlockSpec((1, tn1), lambda j: (0, j)),
            pl.BlockSpec((1, tn1), lambda j: (0, j + nj1)),
        ],
        out_specs=[
            pl.BlockSpec((tm, tn1), lambda j: (0, j)),
            pl.BlockSpec((d, tn1), lambda j: (0, j)),
            pl.BlockSpec((d, tn1), lambda j: (0, j)),
        ],
        out_shape=[jax.ShapeDtypeStruct((tm, d), BF16), jax.ShapeDtypeStruct((d, d), BF16),
                   jax.ShapeDtypeStruct((d, d), BF16)],
        compiler_params=_cparams(("arbitrary",), vmem1),
        name="branch_merge_first",
    )(y, o, w_m, w_g, proj, proj, gb, gb)
    if t == tm:
        return out0
    tn = 512
    jm, jg = PROJ_GM // tn, PROJ_GG // tn
    nj = d // tn
    vmem = 2 * (2 * tm * d * 2 + 2 * d * tn * 2 + 4 * tm * tn * 2) + 6 * tm * tn * 4 + (4 << 20)
    return pl.pallas_call(
        _branch_rest_kernel,
        grid=(t // tm, nj),
        in_specs=[
            pl.BlockSpec((tm, d), lambda i, j: (i, 0)),
            pl.BlockSpec((tm, d), lambda i, j: (i, 0)),
            pl.BlockSpec((d, tn), lambda i, j: (0, j)),
            pl.BlockSpec((d, tn), lambda i, j: (0, j)),
            pl.BlockSpec((tm, tn), lambda i, j: (i, j + jm)),
            pl.BlockSpec((tm, tn), lambda i, j: (i, j + jg)),
            pl.BlockSpec((1, tn), lambda i, j: (0, j)),
            pl.BlockSpec((1, tn), lambda i, j: (0, j + nj)),
            pl.BlockSpec((tm, tn), lambda i, j: (0, _tile0_col(i, j))),
        ],
        out_specs=pl.BlockSpec((tm, tn), lambda i, j: (i, j)),
        out_shape=jax.ShapeDtypeStruct((t, d), BF16),
        compiler_params=_cparams(("parallel", "arbitrary"), vmem),
        name="branch_merge",
    )(y, o, wmb, wgb, proj, proj, gb, gb, out0)


def _conv_silu(x_ref, pad_ref, w_ref, b_ref, rows):
    pad_ref[SUBLANES:SUBLANES + rows, :] = x_ref[...].astype(F32)
    acc = w_ref[3:4, :] * pad_ref[SUBLANES:SUBLANES + rows, :]
    for k in range(3):
        off = SUBLANES - 3 + k
        acc = acc + w_ref[k:k + 1, :] * pad_ref[off:off + rows, :]
    if b_ref is not None:
        acc = acc + b_ref[...]
    pad_ref[0:SUBLANES, :] = pad_ref[rows:rows + SUBLANES, :]
    return _silu(acc)


def _ssd_kernel(xs_ref, bc_ref, z_ref, sm_ref, cwx_ref, cbx_ref, cwbc_ref, cbbc_ref, dtb_ref, alog_ref,
                drep_ref, nw_ref, e64_ref, e128_ref,
                y_ref,
                st_ref, xpad_ref, bcpad_ref, xact_ref, xbf_ref, xw_ref, bcact_ref, larep_ref, exprep_ref,
                lat_ref, dtt_ref, yacc_ref):
    q = SSM_CHUNK
    hg = SSM_HEADS // SSM_GROUPS
    gw = hg * SSM_HEAD_DIM

    @pl.when(pl.program_id(1) == 0)
    def _():
        st_ref[...] = jnp.zeros_like(st_ref)
        xpad_ref[0:SUBLANES, :] = jnp.zeros((SUBLANES, D_SSM), F32)
        bcpad_ref[0:SUBLANES, :] = jnp.zeros((SUBLANES, D_BC), F32)

    xact = _conv_silu(xs_ref, xpad_ref, cwx_ref, cbx_ref, q)
    xact_ref[...] = xact
    xbf_ref[...] = xact.astype(BF16)
    bcact_ref[...] = _conv_silu(bc_ref, bcpad_ref, cwbc_ref, cbbc_ref, q)

    row = lax.broadcasted_iota(jnp.int32, (q, q), 0)
    col = lax.broadcasted_iota(jnp.int32, (q, q), 1)
    causal = row >= col
    tril = jnp.where(causal, 1.0, 0.0).astype(BF16)

    head_lane = col < SSM_HEADS
    dt = jnp.where(head_lane, _softplus(sm_ref[...] + dtb_ref[...]), 0.0)
    a = -jnp.exp(alog_ref[...])
    la = _dot_split_rhs(tril, dt * a, 3)
    la_hi, la_lo = _two_term(la)
    la = la_hi.astype(F32) + la_lo.astype(F32)
    la_last = la[q - 1:q, :]
    larep_ref[...] = _replicate_dot(la, e128_ref[...])
    exprep_ref[...] = _replicate_dot(jnp.exp(la), e64_ref[...])
    to_end = jnp.exp(la_last - la) * dt
    xw_ref[...] = (xact * _replicate_dot(to_end, e64_ref[...])).astype(BF16)
    lat_ref[...] = la.T
    dtt_ref[...] = dt.T

    def group_body(g, carry):
        c0 = pl.multiple_of(g * SSM_STATE, SSM_STATE)
        x0 = pl.multiple_of(g * gw, gw)
        bg = bcact_ref[:, pl.ds(c0, SSM_STATE)]
        cg = bcact_ref[:, pl.ds(SSM_GROUPS * SSM_STATE + c0, SSM_STATE)].astype(BF16)
        cb = _dot_nt(cg, bg.astype(BF16))
        st_g = st_ref[:, pl.ds(x0, gw)]
        dec_g = exprep_ref[:, pl.ds(x0, gw)]
        y_inter = jnp.dot(cg, st_g.astype(BF16), preferred_element_type=F32) * dec_g
        st_ref[:, pl.ds(x0, gw)] = st_g * dec_g[q - 1:q, :] + jnp.dot(
            bg.T.astype(BF16), xw_ref[:, pl.ds(x0, gw)], preferred_element_type=F32)
        for pp in range(hg // 2):
            xp0 = pl.multiple_of(x0 + pp * LANES, LANES)
            xpair = xbf_ref[:, pl.ds(xp0, LANES)]
            res = []
            for e in range(2):
                h = g * hg + pp * 2 + e
                seg = larep_ref[:, pl.ds(pl.multiple_of(h * q, q), q)] - lat_ref[pl.ds(h, 1), :]
                dec = jnp.where(causal, jnp.exp(seg), 0.0)
                lmat = (cb * dec * dtt_ref[pl.ds(h, 1), :]).astype(BF16)
                res.append(jnp.dot(lmat, xpair, preferred_element_type=F32))
            y_intra = jnp.where(col < SSM_HEAD_DIM, res[0], res[1])
            yacc_ref[:, pl.ds(xp0, LANES)] = y_intra + y_inter[:, pp * LANES:(pp + 1) * LANES]
        return carry

    lax.fori_loop(0, SSM_GROUPS, group_body, 0)

    z = z_ref[...].astype(F32)
    y = (yacc_ref[...] + drep_ref[...] * xact_ref[...]) * _silu(z)
    for g in range(SSM_GROUPS):
        yg = y[:, g * gw:(g + 1) * gw]
        ms = jnp.mean(yg * yg, axis=-1, keepdims=True)
        y_ref[:, g * gw:(g + 1) * gw] = (yg * lax.rsqrt(ms + NORM_EPS) * nw_ref[:, g * gw:(g + 1) * gw]).astype(y_ref.dtype)


def _ssd(proj, small, conv_w, conv_b, dt_bias, a_log, d_skip, norm_w, bsz, tb):
    q = SSM_CHUNK
    nc = tb // q
    t = bsz * tb
    pad = LANES - SSM_HEADS
    dtb = jnp.pad(dt_bias, (0, pad)).reshape(1, LANES)
    alog = jnp.pad(a_log, (0, pad)).reshape(1, LANES)
    drep = jnp.repeat(d_skip, SSM_HEAD_DIM).reshape(1, D_SSM)
    r = jnp.arange(2 * LANES)[:, None] % LANES
    e64 = (jnp.arange(D_SSM)[None, :] // SSM_HEAD_DIM == r).astype(BF16)
    e128 = (jnp.arange(SSM_HEADS * q)[None, :] // q == r).astype(BF16)
    const = lambda b, c: (0, 0)
    rowblk = lambda off: (lambda b, c: (b * nc + c, off))
    return pl.pallas_call(
        _ssd_kernel,
        grid=(bsz, nc),
        in_specs=[
            pl.BlockSpec((q, D_SSM), rowblk(PROJ_XS // D_SSM)),
            pl.BlockSpec((q, D_BC), rowblk(PROJ_BC // D_BC)),
            pl.BlockSpec((q, D_SSM), rowblk(PROJ_ZM // D_SSM)),
            pl.BlockSpec((q, LANES), rowblk(0)),
            pl.BlockSpec((SSM_CONV, D_SSM), const),
            pl.BlockSpec((1, D_SSM), const),
            pl.BlockSpec((SSM_CONV, D_BC), const),
            pl.BlockSpec((1, D_BC), const),
            pl.BlockSpec((1, LANES), const),
            pl.BlockSpec((1, LANES), const),
            pl.BlockSpec((1, D_SSM), const),
            pl.BlockSpec((1, D_SSM), const),
            pl.BlockSpec((2 * LANES, D_SSM), const),
            pl.BlockSpec((2 * LANES, SSM_HEADS * q), const),
        ],
        out_specs=pl.BlockSpec((q, D_SSM), rowblk(0)),
        out_shape=jax.ShapeDtypeStruct((t, D_SSM), BF16),
        scratch_shapes=[
            pltpu.VMEM((SSM_STATE, D_SSM), F32),
            pltpu.VMEM((q + SUBLANES, D_SSM), F32),
            pltpu.VMEM((q + SUBLANES, D_BC), F32),
            pltpu.VMEM((q, D_SSM), F32),
            pltpu.VMEM((q, D_SSM), BF16),
            pltpu.VMEM((q, D_SSM), BF16),
            pltpu.VMEM((q, D_BC), F32),
            pltpu.VMEM((q, SSM_HEADS * q), F32),
            pltpu.VMEM((q, D_SSM), F32),
            pltpu.VMEM((LANES, q), F32),
            pltpu.VMEM((LANES, q), F32),
            pltpu.VMEM((q, D_SSM), F32),
        ],
        compiler_params=_cparams(("parallel", "arbitrary"), 56 << 20),
        name="ssd_scan",
    )(proj, proj, proj, small, conv_w[:, :D_SSM], conv_b[:D_SSM].reshape(1, D_SSM), conv_w[:, D_SSM:],
      conv_b[D_SSM:].reshape(1, D_BC), dtb, alog, drep, norm_w.reshape(1, D_SSM), e64, e128)


def _unit_lower_inverse_minus_eye(ms, row, col):
    blk16 = (row ^ col) < 16
    blk32 = (row ^ col) < 32
    ps = [jnp.where(blk16, m, 0.0) for m in ms]
    ns = [-p for p in ps]
    for _ in range(3):
        ps = [_bdot(p, p) for p in ps]
        ns = [n + p + _bdot(n, p) for n, p in zip(ns, ps)]
    for sel in (blk32 & ~blk16, ~blk32):
        offs = [jnp.where(sel, m, 0.0) for m in ms]
        us = [off + _bdot(n, off) for n, off in zip(ns, offs)]
        ns = [n - (u + _bdot(u, n)) for n, u in zip(ns, us)]
    return ns


def _gdn_kernel(q_ref, k_ref, v_ref, z_ref, sm_ref, cwq_ref, cwk_ref, cwv_ref, dtb_ref, alog_ref, nw_ref,
                eg_ref, eb_ref,
                o_ref,
                s_ref, qpad_ref, kpad_ref, vpad_ref, qn_ref, kn_ref, vact_ref, gcrep_ref, betarep_ref, xt_ref):
    c = GDN_CHUNK
    rep = GDN_V_HEADS // GDN_QK_HEADS

    @pl.when(pl.program_id(1) == 0)
    def _():
        s_ref[...] = jnp.zeros_like(s_ref)
        qpad_ref[0:SUBLANES, :] = jnp.zeros((SUBLANES, D_QK), F32)
        kpad_ref[0:SUBLANES, :] = jnp.zeros((SUBLANES, D_QK), F32)
        vpad_ref[0:SUBLANES, :] = jnp.zeros((SUBLANES, D_V), F32)

    qact = _conv_silu(q_ref, qpad_ref, cwq_ref, None, c)
    kact = _conv_silu(k_ref, kpad_ref, cwk_ref, None, c)
    vact_ref[...] = _conv_silu(v_ref, vpad_ref, cwv_ref, None, c)
    for h in range(GDN_QK_HEADS):
        sl = slice(h * GDN_DK, (h + 1) * GDN_DK)
        qh = qact[:, sl]
        kh = kact[:, sl]
        qn_ref[:, sl] = qh * lax.rsqrt(jnp.sum(qh * qh, axis=-1, keepdims=True) + NORM_EPS) * (GDN_DK ** -0.5)
        kn_ref[:, sl] = kh * lax.rsqrt(jnp.sum(kh * kh, axis=-1, keepdims=True) + NORM_EPS)

    row = lax.broadcasted_iota(jnp.int32, (c, c), 0)
    col = lax.broadcasted_iota(jnp.int32, (c, c), 1)
    incl = row >= col
    strict = row > col
    tril = jnp.where(incl, 1.0, 0.0).astype(BF16)

    sm = sm_ref[...]
    lane = lax.broadcasted_iota(jnp.int32, (c, LANES), 1)
    beta = jnp.where((lane >= 64) & (lane < 96), jax.nn.sigmoid(sm), 0.0)
    g = jnp.where(lane >= 96, -jnp.exp(alog_ref[...]) * _softplus(sm + dtb_ref[...]), 0.0)
    gc = _dot_split_rhs(tril, g, 3)
    gc_hi, gc_lo = _two_term(gc)
    gc = gc_hi.astype(F32) + gc_lo.astype(F32)
    gcrep_ref[...] = _replicate_dot(gc, eg_ref[...])
    betarep_ref[...] = _replicate_dot(beta, eb_ref[...])
    xt_ref[...] = jnp.concatenate([gc, jnp.zeros_like(gc)], axis=0).T

    def heads_body(it, carry):
        nv = GDN_HEADS_PER_ITER
        vheads = [it * nv + e for e in range(nv)]
        chs = [pl.multiple_of(h * GDN_DK, GDN_DK) for h in vheads]
        qhs, khs, kkts, qkts = [], [], [], []
        for e in range(nv // rep):
            cq = pl.multiple_of((it * (nv // rep) + e) * GDN_DK, GDN_DK)
            qh = qn_ref[:, pl.ds(cq, GDN_DK)]
            kh = kn_ref[:, pl.ds(cq, GDN_DK)]
            kb = kh.astype(BF16)
            kkt = _dot_nt(kb, kb)
            qkt = _dot_nt(qh.astype(BF16), kb)
            for _ in range(rep):
                qhs.append(qh)
                khs.append(kh)
                kkts.append(kkt)
                qkts.append(qkt)
        gcols = [gcrep_ref[:, pl.ds(ch, GDN_DK)] for ch in chs]
        bcols = [betarep_ref[:, pl.ds(ch, GDN_DK)] for ch in chs]
        grows = [xt_ref[pl.ds(96 + h, 1), :][:, :c] for h in vheads]
        gams = [jnp.where(incl, jnp.exp(gcol[:, :c] - grow), 0.0) for gcol, grow in zip(gcols, grows)]
        ms = [jnp.where(strict, kkt * gam * bcol[:, :c], 0.0) for kkt, gam, bcol in zip(kkts, gams, bcols)]
        ns = _unit_lower_inverse_minus_eye(ms, row, col)
        egs = [jnp.exp(gcol) for gcol in gcols]
        rhss = [jnp.concatenate([vact_ref[:, pl.ds(ch, GDN_DK)] * bcol, kh * (bcol * eg)], axis=1)
                for ch, bcol, kh, eg in zip(chs, bcols, khs, egs)]
        sols = [rhs + _bdot(n, rhs) for n, rhs in zip(ns, rhss)]
        ss = [s_ref[h] for h in vheads]
        sbs = [s.astype(BF16) for s in ss]
        vbs = [(sol[:, :GDN_DK] - jnp.dot(sol[:, GDN_DK:].astype(BF16), sb, preferred_element_type=F32)).astype(BF16)
               for sol, sb in zip(sols, sbs)]
        os_ = [jnp.dot((qh * eg).astype(BF16), sb, preferred_element_type=F32)
               + jnp.dot((qkt * gam).astype(BF16), vb, preferred_element_type=F32)
               for qh, eg, sb, qkt, gam, vb in zip(qhs, egs, sbs, qkts, gams, vbs)]
        for h, s, gcol, kh, vb in zip(vheads, ss, gcols, khs, vbs):
            glast = gcol[c - 1:c, :]
            kdec = kh * jnp.exp(glast - gcol)
            s_ref[h] = s * jnp.exp(glast) + lax.dot_general(
                kdec.astype(BF16), vb, (((0,), (0,)), ((), ())), preferred_element_type=F32)
        for ch, o in zip(chs, os_):
            msq = jnp.mean(o * o, axis=-1, keepdims=True)
            z = z_ref[:, pl.ds(ch, GDN_DK)].astype(F32)
            o_ref[:, pl.ds(ch, GDN_DK)] = (o * lax.rsqrt(msq + NORM_EPS) * nw_ref[...] * _silu(z)).astype(o_ref.dtype)
        return carry

    lax.fori_loop(0, GDN_V_HEADS // GDN_HEADS_PER_ITER, heads_body, 0)


def _gdn(proj, small, conv_w, dt_bias, a_log, norm_w, bsz, tb):
    c = GDN_CHUNK
    nc = tb // c
    t = bsz * tb
    dtb = jnp.pad(dt_bias, (LANES - GDN_V_HEADS, 0)).reshape(1, LANES)
    alog = jnp.pad(a_log, (LANES - GDN_V_HEADS, 0)).reshape(1, LANES)
    r = jnp.arange(2 * LANES)[:, None] % LANES
    head_of_col = jnp.arange(D_V)[None, :] // GDN_DK
    eg = (r == 96 + head_of_col).astype(BF16)
    eb = (r == 64 + head_of_col).astype(BF16)
    const = lambda b, i: (0, 0)
    rowblk = lambda off: (lambda b, i: (b * nc + i, off))
    return pl.pallas_call(
        _gdn_kernel,
        grid=(bsz, nc),
        in_specs=[
            pl.BlockSpec((c, D_QK), rowblk(PROJ_Q // D_QK)),
            pl.BlockSpec((c, D_QK), rowblk(PROJ_K // D_QK)),
            pl.BlockSpec((c, D_V), rowblk(PROJ_V // D_V)),
            pl.BlockSpec((c, D_V), rowblk(PROJ_ZG // D_V)),
            pl.BlockSpec((c, LANES), rowblk(0)),
            pl.BlockSpec((GDN_CONV, D_QK), const),
            pl.BlockSpec((GDN_CONV, D_QK), const),
            pl.BlockSpec((GDN_CONV, D_V), const),
            pl.BlockSpec((1, LANES), const),
            pl.BlockSpec((1, LANES), const),
            pl.BlockSpec((1, GDN_DK), const),
            pl.BlockSpec((2 * LANES, D_V), const),
            pl.BlockSpec((2 * LANES, D_V), const),
        ],
        out_specs=pl.BlockSpec((c, D_V), rowblk(0)),
        out_shape=jax.ShapeDtypeStruct((t, D_V), BF16),
        scratch_shapes=[
            pltpu.VMEM((GDN_V_HEADS, GDN_DK, GDN_DK), F32),
            pltpu.VMEM((c + SUBLANES, D_QK), F32),
            pltpu.VMEM((c + SUBLANES, D_QK), F32),
            pltpu.VMEM((c + SUBLANES, D_V), F32),
            pltpu.VMEM((c, D_QK), F32),
            pltpu.VMEM((c, D_QK), F32),
            pltpu.VMEM((c, D_V), F32),
            pltpu.VMEM((c, D_V), F32),
            pltpu.VMEM((c, D_V), F32),
            pltpu.VMEM((LANES, LANES), F32),
        ],
        compiler_params=_cparams(("parallel", "arbitrary"), 48 << 20),
        name="gdn_scan",
    )(proj, proj, proj, proj, small, conv_w[:, :D_QK], conv_w[:, D_QK:2 * D_QK], conv_w[:, 2 * D_QK:],
      dtb, alog, norm_w.reshape(1, GDN_DK), eg, eb)


def _final_norm_kernel(a_ref, b_ref, g_ref, o_ref):
    x = jnp.concatenate([a_ref[N_META:, :], b_ref[...]], axis=0)
    ms = jnp.mean(x * x, axis=-1, keepdims=True)
    o_ref[...] = (x * lax.rsqrt(ms + NORM_EPS) * g_ref[...]).astype(o_ref.dtype)


def _final_norm(h, gain, bsz, seq, tb):
    d = h.shape[1]
    r = 128
    nb = seq // r
    return pl.pallas_call(
        _final_norm_kernel,
        grid=(bsz, nb),
        in_specs=[
            pl.BlockSpec((r, d), lambda b, i: (b * (tb // r) + i, 0)),
            pl.BlockSpec((N_META, d), lambda b, i: ((b * tb + (i + 1) * r) // N_META, 0)),
            pl.BlockSpec((1, d), lambda b, i: (0, 0)),
        ],
        out_specs=pl.BlockSpec((None, r, d), lambda b, i: (b, i, 0)),
        out_shape=jax.ShapeDtypeStruct((bsz, seq, d), F32),
        compiler_params=_cparams(("parallel", "parallel"), 4 * r * d * 4 * 2 + (8 << 20)),
        name="final_norm",
    )(h, h, gain.reshape(1, d))


def _rearranged_in_proj(w_in):
    o = 0
    parts = {}
    for name, width in (("zm", D_SSM), ("xs", D_SSM), ("bc", D_BC), ("dt", SSM_HEADS), ("q", D_QK), ("k", D_QK),
                        ("v", D_V), ("zg", D_V), ("b", GDN_V_HEADS), ("a", GDN_V_HEADS), ("gm", D_MODEL),
                        ("gg", D_MODEL)):
        parts[name] = w_in[:, o:o + width]
        o += width
    big = jnp.concatenate([parts[n] for n in ("zm", "xs", "zg", "v", "gm", "gg", "bc", "q", "k")], axis=1)
    small = jnp.concatenate([parts["dt"], parts["b"], parts["a"]], axis=1)
    return big.astype(BF16), small.astype(BF16)


def kernel(x, meta_tokens, ffn1_norm, ffn1_w_gate_up, ffn1_w_down, mix_norm, w_in, ssm_conv_w, ssm_conv_b,
           ssm_dt_bias, ssm_a_log, ssm_d, ssm_norm, ssm_w_out, gdn_conv_w, gdn_dt_bias, gdn_a_log, gdn_norm,
           gdn_w_out, gate_b, w_o, ffn2_norm, ffn2_w_gate_up, ffn2_w_down, final_norm):
    bsz, seq, d = x.shape
    ltot = seq + N_META
    tb = -(-ltot // SSM_CHUNK) * SSM_CHUNK
    depth = w_in.shape[0]
    h, hg, ssq = _embed_prep(x, meta_tokens, ffn1_norm[0], tb)
    for i in range(depth):
        a = _ffn_up(hg, ssq, ffn1_w_gate_up, i)
        h, hg, ssq = _mm_res(a, ffn1_w_down, i, h, mix_norm[i])
        w_big, w_small = _rearranged_in_proj(w_in[i])
        proj, small = _inproj(hg, ssq, w_big, w_small)
        y = _ssd(proj, small, ssm_conv_w[i], ssm_conv_b[i], ssm_dt_bias[i], ssm_a_log[i], ssm_d[i], ssm_norm[i],
                 bsz, tb)
        o = _gdn(proj, small, gdn_conv_w[i], gdn_dt_bias[i], gdn_a_log[i], gdn_norm[i], bsz, tb)
        merged = _branch_merge(y, o, ssm_w_out, gdn_w_out, i, proj, gate_b[i])
        h, hg, ssq = _mm_res(merged, w_o, i, h, ffn2_norm[i])
        a = _ffn_up(hg, ssq, ffn2_w_gate_up, i)
        if i + 1 < depth:
            h, hg, ssq = _mm_res(a, ffn2_w_down, i, h, ffn1_norm[i + 1])
        else:
            h = _mm_res(a, ffn2_w_down, i, h, None)
    return _final_norm(h, final_norm, bsz, seq, tb)
```

```python
import functools

import jax
import jax.numpy as jnp
from jax import lax
from jax.experimental import pallas as pl
from jax.experimental.pallas import tpu as pltpu

F32 = jnp.float32
BF16 = jnp.bfloat16

D_MODEL = 4096
N_META = 16
NORM_EPS = 1e-6
D_FF = 2 * D_MODEL
SSM_HEADS = 64
SSM_HEAD_DIM = 64
SSM_GROUPS = 8
SSM_STATE = 128
SSM_CONV = 4
SSM_CHUNK = 128
SSM_GROUPS_PER_ITER = 4
D_SSM = SSM_HEADS * SSM_HEAD_DIM
D_BC = 2 * SSM_GROUPS * SSM_STATE
GDN_DK = 128
GDN_QK_HEADS = 16
GDN_V_HEADS = 32
GDN_CONV = 4
GDN_CHUNK = 64
GDN_HEADS_PER_ITER = 32
D_QK = GDN_QK_HEADS * GDN_DK
D_V = GDN_V_HEADS * GDN_DK

LANES = 128
SUBLANES = 8
VMEM_CAP = 60 * 1024 * 1024

PROJ_ZM, PROJ_XS, PROJ_ZG, PROJ_V, PROJ_GM, PROJ_GG = 0, 4096, 8192, 12288, 16384, 20480
PROJ_BC, PROJ_Q, PROJ_K = 24576, 26624, 28672
N_PROJ = 30720


def _cparams(sem, vmem_bytes):
    return pltpu.CompilerParams(dimension_semantics=sem, vmem_limit_bytes=min(int(vmem_bytes), VMEM_CAP))


def _pick_tile(n, candidates):
    for c in candidates:
        if n % c == 0:
            return c
    raise ValueError(f"no tile for {n} in {candidates}")


def _silu(x):
    return x * jax.nn.sigmoid(x)


def _softplus(x):
    return jnp.maximum(x, 0.0) + jnp.log(1.0 + jnp.exp(-jnp.abs(x)))


def _split_bf16(x, passes):
    parts = []
    r = x
    for p in range(passes):
        b = r.astype(BF16)
        parts.append(b)
        if p + 1 < passes:
            r = r - b.astype(F32)
    return parts


def _dot_split_rhs(a_bf16, x, passes):
    acc = None
    for p in _split_bf16(x, passes):
        d = jnp.dot(a_bf16, p, preferred_element_type=F32)
        acc = d if acc is None else acc + d
    return acc


def _two_term(x):
    hi = x.astype(BF16)
    return hi, (x - hi.astype(F32)).astype(BF16)


def _replicate_dot(x, e2_bf16):
    hi, lo = _two_term(x)
    return jnp.dot(jnp.concatenate([hi, lo], axis=1), e2_bf16, preferred_element_type=F32)


def _dot_nt(a, b):
    return lax.dot_general(a, b, (((1,), (1,)), ((), ())), preferred_element_type=F32)


def _bdot(a, b):
    return jnp.dot(a.astype(BF16), b.astype(BF16), preferred_element_type=F32)


def _lane_partial_sumsq(x):
    sq = x * x
    acc = sq[:, 0:LANES]
    for k in range(1, x.shape[1] // LANES):
        acc = acc + sq[:, k * LANES:(k + 1) * LANES]
    return acc


def _row_rscale(ssq):
    return lax.rsqrt(jnp.sum(ssq, axis=-1, keepdims=True) * (1.0 / D_MODEL) + NORM_EPS)


def _embed_prep_kernel(xa_ref, xb_ref, meta_ref, g_ref, h_ref, hg_ref, ssq_ref, *, nxb):
    r = pl.program_id(1)
    rows = h_ref.shape[0]
    top = jnp.where(r == 0, meta_ref[...], jnp.where(r <= nxb, xb_ref[...], 0.0))
    bot = jnp.where(r < nxb, xa_ref[0:rows - N_META, :], 0.0)
    hb = jnp.concatenate([top, bot], axis=0)
    h_ref[...] = hb
    hg_ref[...] = (hb * g_ref[...]).astype(hg_ref.dtype)
    ssq_ref[...] = _lane_partial_sumsq(hb)


def _embed_prep(x, meta_tokens, gain, tb):
    bsz, seq, d = x.shape
    rows = SSM_CHUNK
    nxb = seq // rows
    nrb = tb // rows
    per16 = rows // N_META
    t = bsz * tb
    return pl.pallas_call(
        functools.partial(_embed_prep_kernel, nxb=nxb),
        grid=(bsz, nrb),
        in_specs=[
            pl.BlockSpec((None, rows, d), lambda b, r: (b, jnp.minimum(r, nxb - 1), 0)),
            pl.BlockSpec((None, N_META, d), lambda b, r: (b, jnp.clip(r * per16 - 1, 0, seq // N_META - 1), 0)),
            pl.BlockSpec((N_META, d), lambda b, r: (0, 0)),
            pl.BlockSpec((1, d), lambda b, r: (0, 0)),
        ],
        out_specs=[
            pl.BlockSpec((rows, d), lambda b, r: (b * nrb + r, 0)),
            pl.BlockSpec((rows, d), lambda b, r: (b * nrb + r, 0)),
            pl.BlockSpec((rows, LANES), lambda b, r: (b * nrb + r, 0)),
        ],
        out_shape=[jax.ShapeDtypeStruct((t, d), F32), jax.ShapeDtypeStruct((t, d), BF16),
                   jax.ShapeDtypeStruct((t, LANES), F32)],
        compiler_params=_cparams(("parallel", "parallel"), 4 * rows * d * (4 + 4 + 4 + 2) + (8 << 20)),
        name="embed_prep",
    )(x, x, meta_tokens.astype(x.dtype), gain.reshape(1, d))


def _tile0_col(i, j):
    return jnp.where(i == 0, j, 0)


def _skip0_col(i, j):
    return jnp.where(i == 0, 0, j)


def _ffn_up_math(x, r, wg, wu):
    g = jnp.dot(x, wg, preferred_element_type=F32) * r
    u = jnp.dot(x, wu, preferred_element_type=F32) * r
    return (_silu(g) * u * 0.5).astype(BF16)


def _ffn_up_first_kernel(x_ref, ssq_ref, wg_ref, wu_ref, o_ref, wgb_ref, wub_ref):
    wg = wg_ref[...].astype(BF16)
    wu = wu_ref[...].astype(BF16)
    wgb_ref[...] = wg
    wub_ref[...] = wu
    o_ref[...] = _ffn_up_math(x_ref[...], _row_rscale(ssq_ref[...]), wg, wu)


def _ffn_up_rest_kernel(x_ref, ssq_ref, wg_ref, wu_ref, o0_ref, o_ref):
    i = pl.program_id(0)

    @pl.when(i == 0)
    def _():
        o_ref[...] = o0_ref[...]

    @pl.when(i > 0)
    def _():
        o_ref[...] = _ffn_up_math(x_ref[...], _row_rscale(ssq_ref[...]), wg_ref[...], wu_ref[...])


def _ffn_up(hg, ssq, w_gate_up, layer):
    t, d = hg.shape
    tm = _pick_tile(t, (1280, 640, 256, 128))
    tn1 = 256
    nj1 = D_FF // tn1
    vmem1 = 2 * (tm * d * 2 + tm * LANES * 4 + 2 * d * tn1 * (4 + 2) + tm * tn1 * 2) + 6 * tm * tn1 * 4 + (4 << 20)
    a0, wgb, wub = pl.pallas_call(
        _ffn_up_first_kernel,
        grid=(nj1,),
        in_specs=[
            pl.BlockSpec((tm, d), lambda j: (0, 0)),
            pl.BlockSpec((tm, LANES), lambda j: (0, 0)),
            pl.BlockSpec((None, d, tn1), lambda j: (layer, 0, j)),
            pl.BlockSpec((None, d, tn1), lambda j: (layer, 0, j + nj1)),
        ],
        out_specs=[
            pl.BlockSpec((tm, tn1), lambda j: (0, j)),
            pl.BlockSpec((d, tn1), lambda j: (0, j)),
            pl.BlockSpec((d, tn1), lambda j: (0, j)),
        ],
        out_shape=[jax.ShapeDtypeStruct((tm, D_FF), BF16), jax.ShapeDtypeStruct((d, D_FF), BF16),
                   jax.ShapeDtypeStruct((d, D_FF), BF16)],
        compiler_params=_cparams(("arbitrary",), vmem1),
        name="ffn_up_first",
    )(hg, ssq, w_gate_up, w_gate_up)
    if t == tm:
        return a0
    tn = 512
    vmem = 2 * (tm * d * 2 + tm * LANES * 4 + 2 * d * tn * 2 + 2 * tm * tn * 2) + 6 * tm * tn * 4 + (4 << 20)
    return pl.pallas_call(
        _ffn_up_rest_kernel,
        grid=(t // tm, D_FF // tn),
        in_specs=[
            pl.BlockSpec((tm, d), lambda i, j: (i, 0)),
            pl.BlockSpec((tm, LANES), lambda i, j: (i, 0)),
            pl.BlockSpec((d, tn), lambda i, j: (0, _skip0_col(i, j))),
            pl.BlockSpec((d, tn), lambda i, j: (0, _skip0_col(i, j))),
            pl.BlockSpec((tm, tn), lambda i, j: (0, _tile0_col(i, j))),
        ],
        out_specs=pl.BlockSpec((tm, tn), lambda i, j: (i, j)),
        out_shape=jax.ShapeDtypeStruct((t, D_FF), BF16),
        compiler_params=_cparams(("parallel", "arbitrary"), vmem),
        name="ffn_up",
    )(hg, ssq, wgb, wub, a0)


def _mm_res_store(hn, gn_ref, j, o_ref, hg_ref, ssq_ref):
    o_ref[...] = hn
    if hg_ref is None:
        return
    hg_ref[...] = (hn * gn_ref[...]).astype(hg_ref.dtype)
    part = _lane_partial_sumsq(hn)

    @pl.when(j == 0)
    def _():
        ssq_ref[...] = part

    @pl.when(j > 0)
    def _():
        ssq_ref[...] += part


def _mm_res_first_kernel(a_ref, w_ref, h_ref, gn_ref, o_ref, wb_ref, hg_ref=None, ssq_ref=None):
    w = w_ref[...].astype(BF16)
    wb_ref[...] = w
    hn = h_ref[...] + jnp.dot(a_ref[...], w, preferred_element_type=F32)
    _mm_res_store(hn, gn_ref, pl.program_id(0), o_ref, hg_ref, ssq_ref)


def _mm_res_rest_kernel(a_ref, w_ref, h_ref, gn_ref, o0_ref, hg0_ref, ssq0_ref, o_ref, hg_ref, ssq_ref):
    i = pl.program_id(0)

    @pl.when(i == 0)
    def _():
        o_ref[...] = o0_ref[...]
        if hg_ref is not None:
            hg_ref[...] = hg0_ref[...]
            ssq_ref[...] = ssq0_ref[...]

    @pl.when(i > 0)
    def _():
        hn = h_ref[...] + jnp.dot(a_ref[...], w_ref[...], preferred_element_type=F32)
        _mm_res_store(hn, gn_ref, pl.program_id(1), o_ref, hg_ref, ssq_ref)


def _mm_res_rest_plain_kernel(a_ref, w_ref, h_ref, gn_ref, o0_ref, o_ref):
    _mm_res_rest_kernel(a_ref, w_ref, h_ref, gn_ref, o0_ref, None, None, o_ref, None, None)


def _mm_res(a, w, layer, h, next_gain):
    t, kdim = a.shape
    n = w.shape[2]
    emit = next_gain is not None
    gn = (next_gain if emit else jnp.ones((n,), F32)).reshape(1, n)
    tm = _pick_tile(t, (640, 256, 128))
    tn1 = 256
    blk = lambda shape, imap: pl.BlockSpec(shape, imap)
    vmem1 = 2 * (tm * kdim * 2 + kdim * tn1 * (4 + 2) + tm * tn1 * (4 + 4 + 2) + tm * LANES * 4) + 4 * tm * tn1 * 4 + (4 << 20)
    out_specs1 = [blk((tm, tn1), lambda j: (0, j)), blk((kdim, tn1), lambda j: (0, j))]
    out_shape1 = [jax.ShapeDtypeStruct((tm, n), F32), jax.ShapeDtypeStruct((kdim, n), BF16)]
    if emit:
        out_specs1 += [blk((tm, tn1), lambda j: (0, j)), blk((tm, LANES), lambda j: (0, 0))]
        out_shape1 += [jax.ShapeDtypeStruct((tm, n), BF16), jax.ShapeDtypeStruct((tm, LANES), F32)]
    first = pl.pallas_call(
        _mm_res_first_kernel,
        grid=(n // tn1,),
        in_specs=[blk((tm, kdim), lambda j: (0, 0)), blk((None, kdim, tn1), lambda j: (layer, 0, j)),
                  blk((tm, tn1), lambda j: (0, j)), blk((1, tn1), lambda j: (0, j))],
        out_specs=out_specs1,
        out_shape=out_shape1,
        compiler_params=_cparams(("arbitrary",), vmem1),
        name="mm_res_first",
    )(a, w, h, gn)
    if t == tm:
        return (first[0], first[2], first[3]) if emit else first[0]
    tn = 512 if kdim > 4096 else 1024
    vmem = (2 * (tm * kdim * 2 + kdim * tn * 2 + tm * tn * (4 + 4 + 2 + 4 + 2) + 2 * tm * LANES * 4)
            + 3 * tm * tn * 4 + (4 << 20))
    in_specs = [blk((tm, kdim), lambda i, j: (i, 0)), blk((kdim, tn), lambda i, j: (0, _skip0_col(i, j))),
                blk((tm, tn), lambda i, j: (i, _skip0_col(i, j))), blk((1, tn), lambda i, j: (0, j)),
                blk((tm, tn), lambda i, j: (0, _tile0_col(i, j)))]
    out_specs = [blk((tm, tn), lambda i, j: (i, j))]
    out_shape = [jax.ShapeDtypeStruct((t, n), F32)]
    args = [a, first[1], h, gn, first[0]]
    if emit:
        in_specs += [blk((tm, tn), lambda i, j: (0, _tile0_col(i, j))), blk((tm, LANES), lambda i, j: (0, 0))]
        out_specs += [blk((tm, tn), lambda i, j: (i, j)), blk((tm, LANES), lambda i, j: (i, 0))]
        out_shape += [jax.ShapeDtypeStruct((t, n), BF16), jax.ShapeDtypeStruct((t, LANES), F32)]
        args += [first[2], first[3]]
    rest = pl.pallas_call(
        _mm_res_rest_kernel if emit else _mm_res_rest_plain_kernel,
        grid=(t // tm, n // tn),
        in_specs=in_specs,
        out_specs=out_specs,
        out_shape=out_shape,
        compiler_params=_cparams(("parallel", "arbitrary"), vmem),
        name="mm_res",
    )(*args)
    return tuple(rest) if emit else rest[0]


def _inproj_kernel(x_ref, ssq_ref, w_ref, ws_ref, o_ref, os_ref):
    x = x_ref[...]
    r = _row_rscale(ssq_ref[...])
    o_ref[...] = (jnp.dot(x, w_ref[...], preferred_element_type=F32) * r).astype(o_ref.dtype)

    @pl.when(pl.program_id(1) == 0)
    def _():
        os_ref[...] = jnp.dot(x, ws_ref[...], preferred_element_type=F32) * r


def _inproj(hg, ssq, w_big, w_small):
    t, d = hg.shape
    tm = _pick_tile(t, (1280, 640, 256, 128))
    tn = 1024
    vmem = (2 * (tm * d * 2 + tm * LANES * 4 + d * tn * 2 + tm * tn * 2 + d * LANES * 2 + tm * LANES * 4)
            + 2 * tm * tn * 4 + (4 << 20))
    return pl.pallas_call(
        _inproj_kernel,
        grid=(t // tm, N_PROJ // tn),
        in_specs=[
            pl.BlockSpec((tm, d), lambda i, j: (i, 0)),
            pl.BlockSpec((tm, LANES), lambda i, j: (i, 0)),
            pl.BlockSpec((d, tn), lambda i, j: (0, j)),
            pl.BlockSpec((d, LANES), lambda i, j: (0, 0)),
        ],
        out_specs=[
            pl.BlockSpec((tm, tn), lambda i, j: (i, j)),
            pl.BlockSpec((tm, LANES), lambda i, j: (i, 0)),
        ],
        out_shape=[jax.ShapeDtypeStruct((t, N_PROJ), BF16), jax.ShapeDtypeStruct((t, LANES), F32)],
        compiler_params=_cparams(("parallel", "arbitrary"), vmem),
        name="inproj",
    )(hg, ssq, w_big, w_small)


def _branch_math(y, o, wm, wg, gm, gg, bm, bg):
    br_m = jnp.dot(y, wm, preferred_element_type=F32)
    br_g = jnp.dot(o, wg, preferred_element_type=F32)
    gate_m = jax.nn.sigmoid(gm.astype(F32) + bm)
    gate_g = jax.nn.sigmoid(gg.astype(F32) + bg)
    return (gate_m * br_m + gate_g * br_g).astype(BF16)


def _branch_first_kernel(y_ref, o_ref, wm_ref, wg_ref, gm_ref, gg_ref, bm_ref, bg_ref, out_ref, wmb_ref, wgb_ref):
    wm = wm_ref[...].astype(BF16)
    wg = wg_ref[...].astype(BF16)
    wmb_ref[...] = wm
    wgb_ref[...] = wg
    out_ref[...] = _branch_math(y_ref[...], o_ref[...], wm, wg, gm_ref[...], gg_ref[...], bm_ref[...], bg_ref[...])


def _branch_rest_kernel(y_ref, o_ref, wm_ref, wg_ref, gm_ref, gg_ref, bm_ref, bg_ref, out0_ref, out_ref):
    i = pl.program_id(0)

    @pl.when(i == 0)
    def _():
        out_ref[...] = out0_ref[...]

    @pl.when(i > 0)
    def _():
        out_ref[...] = _branch_math(y_ref[...], o_ref[...], wm_ref[...], wg_ref[...], gm_ref[...], gg_ref[...],
                                    bm_ref[...], bg_ref[...])


def _branch_merge(y, o, w_m, w_g, layer, proj, gate_b):
    t, d = y.shape
    tm = _pick_tile(t, (640, 256, 128))
    gb = gate_b.reshape(1, 2 * d)
    tn1 = 256
    nj1 = d // tn1
    vmem1 = 2 * (2 * tm * d * 2 + 2 * d * tn1 * (4 + 2) + 3 * tm * tn1 * 2) + 6 * tm * tn1 * 4 + (4 << 20)
    out0, wmb, wgb = pl.pallas_call(
        _branch_first_kernel,
        grid=(nj1,),
        in_specs=[
            pl.BlockSpec((tm, d), lambda j: (0, 0)),
            pl.BlockSpec((tm, d), lambda j: (0, 0)),
            pl.BlockSpec((None, d, tn1), lambda j: (layer, 0, j)),
            pl.BlockSpec((None, d, tn1), lambda j: (layer, 0, j)),
            pl.BlockSpec((tm, tn1), lambda j: (0, j + PROJ_GM // tn1)),
            pl.BlockSpec((tm, tn1), lambda j: (0, j + PROJ_GG // tn1)),
            pl.BlockSpec((1, tn1), lambda j: (0, j)),
            pl.BlockSpec((1, tn1), lambda j: (0, j + nj1)),
        ],
        out_specs=[
            pl.BlockSpec((tm, tn1), lambda j: (0, j)),
            pl.BlockSpec((d, tn1), lambda j: (0, j)),
            pl.BlockSpec((d, tn1), lambda j: (0, j)),
        ],
        out_shape=[jax.ShapeDtypeStruct((tm, d), BF16), jax.ShapeDtypeStruct((d, d), BF16),
                   jax.ShapeDtypeStruct((d, d), BF16)],
        compiler_params=_cparams(("arbitrary",), vmem1),
        name="branch_merge_first",
    )(y, o, w_m, w_g, proj, proj, gb, gb)
    if t == tm:
        return out0
    tn = 512
    jm, jg = PROJ_GM // tn, PROJ_GG // tn
    nj = d // tn
    vmem = 2 * (2 * tm * d * 2 + 2 * d * tn * 2 + 4 * tm * tn * 2) + 6 * tm * tn * 4 + (4 << 20)
    return pl.pallas_call(
        _branch_rest_kernel,
        grid=(t // tm, nj),
        in_specs=[
            pl.BlockSpec((tm, d), lambda i, j: (i, 0)),
            pl.BlockSpec((tm, d), lambda i, j: (i, 0)),
            pl.BlockSpec((d, tn), lambda i, j: (0, _skip0_col(i, j))),
            pl.BlockSpec((d, tn), lambda i, j: (0, _skip0_col(i, j))),
            pl.BlockSpec((tm, tn), lambda i, j: (i, _skip0_col(i, j) + jm)),
            pl.BlockSpec((tm, tn), lambda i, j: (i, _skip0_col(i, j) + jg)),
            pl.BlockSpec((1, tn), lambda i, j: (0, j)),
            pl.BlockSpec((1, tn), lambda i, j: (0, j + nj)),
            pl.BlockSpec((tm, tn), lambda i, j: (0, _tile0_col(i, j))),
        ],
        out_specs=pl.BlockSpec((tm, tn), lambda i, j: (i, j)),
        out_shape=jax.ShapeDtypeStruct((t, d), BF16),
        compiler_params=_cparams(("parallel", "arbitrary"), vmem),
        name="branch_merge",
    )(y, o, wmb, wgb, proj, proj, gb, gb, out0)


def _conv_silu(x_ref, pad_ref, w_ref, b_ref, rows):
    pad_ref[SUBLANES:SUBLANES + rows, :] = x_ref[...].astype(F32)
    acc = w_ref[3:4, :] * pad_ref[SUBLANES:SUBLANES + rows, :]
    for k in range(3):
        off = SUBLANES - 3 + k
        acc = acc + w_ref[k:k + 1, :] * pad_ref[off:off + rows, :]
    if b_ref is not None:
        acc = acc + b_ref[...]
    pad_ref[0:SUBLANES, :] = pad_ref[rows:rows + SUBLANES, :]
    return _silu(acc)


def _ssd_kernel(xs_ref, bc_ref, z_ref, sm_ref, cwx_ref, cbx_ref, cwbc_ref, cbbc_ref, dtb_ref, alog_ref,
                drep_ref, nw_ref, e64_ref, e128_ref,
                y_ref,
                st_ref, xpad_ref, bcpad_ref, xact_ref, xbf_ref, xw_ref, bcact_ref, larep_ref, exprep_ref,
                lat_ref, dtt_ref, yacc_ref):
    q = SSM_CHUNK
    hg = SSM_HEADS // SSM_GROUPS
    gw = hg * SSM_HEAD_DIM

    @pl.when(pl.program_id(1) == 0)
    def _():
        st_ref[...] = jnp.zeros_like(st_ref)
        xpad_ref[0:SUBLANES, :] = jnp.zeros((SUBLANES, D_SSM), F32)
        bcpad_ref[0:SUBLANES, :] = jnp.zeros((SUBLANES, D_BC), F32)

    xact = _conv_silu(xs_ref, xpad_ref, cwx_ref, cbx_ref, q)
    xact_ref[...] = xact
    xbf_ref[...] = xact.astype(BF16)
    bcact_ref[...] = _conv_silu(bc_ref, bcpad_ref, cwbc_ref, cbbc_ref, q)

    row = lax.broadcasted_iota(jnp.int32, (q, q), 0)
    col = lax.broadcasted_iota(jnp.int32, (q, q), 1)
    causal = row >= col
    tril = jnp.where(causal, 1.0, 0.0).astype(BF16)

    head_lane = col < SSM_HEADS
    dt = jnp.where(head_lane, _softplus(sm_ref[...] + dtb_ref[...]), 0.0)
    a = -jnp.exp(alog_ref[...])
    la = _dot_split_rhs(tril, dt * a, 3)
    la_hi, la_lo = _two_term(la)
    la = la_hi.astype(F32) + la_lo.astype(F32)
    la_last = la[q - 1:q, :]
    larep_ref[...] = _replicate_dot(la, e128_ref[...])
    exprep_ref[...] = _replicate_dot(jnp.exp(la), e64_ref[...])
    to_end = jnp.exp(la_last - la) * dt
    xw_ref[...] = (xact * _replicate_dot(to_end, e64_ref[...])).astype(BF16)
    lat_ref[...] = la.T
    dtt_ref[...] = dt.T

    def one_group(g):
        c0 = pl.multiple_of(g * SSM_STATE, SSM_STATE)
        x0 = pl.multiple_of(g * gw, gw)
        bg = bcact_ref[:, pl.ds(c0, SSM_STATE)]
        cg = bcact_ref[:, pl.ds(SSM_GROUPS * SSM_STATE + c0, SSM_STATE)].astype(BF16)
        cb = _dot_nt(cg, bg.astype(BF16))
        st_g = st_ref[:, pl.ds(x0, gw)]
        dec_g = exprep_ref[:, pl.ds(x0, gw)]
        y_inter = jnp.dot(cg, st_g.astype(BF16), preferred_element_type=F32) * dec_g
        st_ref[:, pl.ds(x0, gw)] = st_g * dec_g[q - 1:q, :] + jnp.dot(
            bg.T.astype(BF16), xw_ref[:, pl.ds(x0, gw)], preferred_element_type=F32)
        for pp in range(hg // 2):
            xp0 = pl.multiple_of(x0 + pp * LANES, LANES)
            xpair = xbf_ref[:, pl.ds(xp0, LANES)]
            res = []
            for e in range(2):
                h = g * hg + pp * 2 + e
                seg = larep_ref[:, pl.ds(pl.multiple_of(h * q, q), q)] - lat_ref[pl.ds(h, 1), :]
                dec = jnp.where(causal, jnp.exp(seg), 0.0)
                lmat = (cb * dec * dtt_ref[pl.ds(h, 1), :]).astype(BF16)
                res.append(jnp.dot(lmat, xpair, preferred_element_type=F32))
            y_intra = jnp.where(col < SSM_HEAD_DIM, res[0], res[1])
            yacc_ref[:, pl.ds(xp0, LANES)] = y_intra + y_inter[:, pp * LANES:(pp + 1) * LANES]

    def groups_body(it, carry):
        for e in range(SSM_GROUPS_PER_ITER):
            one_group(it * SSM_GROUPS_PER_ITER + e)
        return carry

    lax.fori_loop(0, SSM_GROUPS // SSM_GROUPS_PER_ITER, groups_body, 0)

    z = z_ref[...].astype(F32)
    y = (yacc_ref[...] + drep_ref[...] * xact_ref[...]) * _silu(z)
    for g in range(SSM_GROUPS):
        yg = y[:, g * gw:(g + 1) * gw]
        ms = jnp.mean(yg * yg, axis=-1, keepdims=True)
        y_ref[:, g * gw:(g + 1) * gw] = (yg * lax.rsqrt(ms + NORM_EPS) * nw_ref[:, g * gw:(g + 1) * gw]).astype(y_ref.dtype)


def _ssd(proj, small, conv_w, conv_b, dt_bias, a_log, d_skip, norm_w, bsz, tb):
    q = SSM_CHUNK
    nc = tb // q
    t = bsz * tb
    pad = LANES - SSM_HEADS
    dtb = jnp.pad(dt_bias, (0, pad)).reshape(1, LANES)
    alog = jnp.pad(a_log, (0, pad)).reshape(1, LANES)
    drep = jnp.repeat(d_skip, SSM_HEAD_DIM).reshape(1, D_SSM)
    r = jnp.arange(2 * LANES)[:, None] % LANES
    e64 = (jnp.arange(D_SSM)[None, :] // SSM_HEAD_DIM == r).astype(BF16)
    e128 = (jnp.arange(SSM_HEADS * q)[None, :] // q == r).astype(BF16)
    const = lambda b, c: (0, 0)
    rowblk = lambda off: (lambda b, c: (b * nc + c, off))
    return pl.pallas_call(
        _ssd_kernel,
        grid=(bsz, nc),
        in_specs=[
            pl.BlockSpec((q, D_SSM), rowblk(PROJ_XS // D_SSM)),
            pl.BlockSpec((q, D_BC), rowblk(PROJ_BC // D_BC)),
            pl.BlockSpec((q, D_SSM), rowblk(PROJ_ZM // D_SSM)),
            pl.BlockSpec((q, LANES), rowblk(0)),
            pl.BlockSpec((SSM_CONV, D_SSM), const),
            pl.BlockSpec((1, D_SSM), const),
            pl.BlockSpec((SSM_CONV, D_BC), const),
            pl.BlockSpec((1, D_BC), const),
            pl.BlockSpec((1, LANES), const),
            pl.BlockSpec((1, LANES), const),
            pl.BlockSpec((1, D_SSM), const),
            pl.BlockSpec((1, D_SSM), const),
            pl.BlockSpec((2 * LANES, D_SSM), const),
            pl.BlockSpec((2 * LANES, SSM_HEADS * q), const),
        ],
        out_specs=pl.BlockSpec((q, D_SSM), rowblk(0)),
        out_shape=jax.ShapeDtypeStruct((t, D_SSM), BF16),
        scratch_shapes=[
            pltpu.VMEM((SSM_STATE, D_SSM), F32),
            pltpu.VMEM((q + SUBLANES, D_SSM), F32),
            pltpu.VMEM((q + SUBLANES, D_BC), F32),
            pltpu.VMEM((q, D_SSM), F32),
            pltpu.VMEM((q, D_SSM), BF16),
            pltpu.VMEM((q, D_SSM), BF16),
            pltpu.VMEM((q, D_BC), F32),
            pltpu.VMEM((q, SSM_HEADS * q), F32),
            pltpu.VMEM((q, D_SSM), F32),
            pltpu.VMEM((LANES, q), F32),
            pltpu.VMEM((LANES, q), F32),
            pltpu.VMEM((q, D_SSM), F32),
        ],
        compiler_params=_cparams(("parallel", "arbitrary"), 56 << 20),
        name="ssd_scan",
    )(proj, proj, proj, small, conv_w[:, :D_SSM], conv_b[:D_SSM].reshape(1, D_SSM), conv_w[:, D_SSM:],
      conv_b[D_SSM:].reshape(1, D_BC), dtb, alog, drep, norm_w.reshape(1, D_SSM), e64, e128)


def _unit_lower_inverse_minus_eye(ms, row, col):
    blk16 = (row ^ col) < 16
    blk32 = (row ^ col) < 32
    ps = [jnp.where(blk16, m, 0.0) for m in ms]
    ns = [-p for p in ps]
    for _ in range(3):
        ps = [_bdot(p, p) for p in ps]
        ns = [n + p + _bdot(n, p) for n, p in zip(ns, ps)]
    for sel in (blk32 & ~blk16, ~blk32):
        offs = [jnp.where(sel, m, 0.0) for m in ms]
        us = [off + _bdot(n, off) for n, off in zip(ns, offs)]
        ns = [n - (u + _bdot(u, n)) for n, u in zip(ns, us)]
    return ns


def _gdn_kernel(q_ref, k_ref, v_ref, z_ref, sm_ref, cwq_ref, cwk_ref, cwv_ref, dtb_ref, alog_ref, nw_ref,
                eg_ref, eb_ref,
                o_ref,
                s_ref, qpad_ref, kpad_ref, vpad_ref, qn_ref, kn_ref, vact_ref, gcrep_ref, betarep_ref, xt_ref):
    c = GDN_CHUNK
    rep = GDN_V_HEADS // GDN_QK_HEADS

    @pl.when(pl.program_id(1) == 0)
    def _():
        s_ref[...] = jnp.zeros_like(s_ref)
        qpad_ref[0:SUBLANES, :] = jnp.zeros((SUBLANES, D_QK), F32)
        kpad_ref[0:SUBLANES, :] = jnp.zeros((SUBLANES, D_QK), F32)
        vpad_ref[0:SUBLANES, :] = jnp.zeros((SUBLANES, D_V), F32)

    qact = _conv_silu(q_ref, qpad_ref, cwq_ref, None, c)
    kact = _conv_silu(k_ref, kpad_ref, cwk_ref, None, c)
    vact_ref[...] = _conv_silu(v_ref, vpad_ref, cwv_ref, None, c)
    for h in range(GDN_QK_HEADS):
        sl = slice(h * GDN_DK, (h + 1) * GDN_DK)
        qh = qact[:, sl]
        kh = kact[:, sl]
        qn_ref[:, sl] = qh * lax.rsqrt(jnp.sum(qh * qh, axis=-1, keepdims=True) + NORM_EPS) * (GDN_DK ** -0.5)
        kn_ref[:, sl] = kh * lax.rsqrt(jnp.sum(kh * kh, axis=-1, keepdims=True) + NORM_EPS)

    row = lax.broadcasted_iota(jnp.int32, (c, c), 0)
    col = lax.broadcasted_iota(jnp.int32, (c, c), 1)
    incl = row >= col
    strict = row > col
    tril = jnp.where(incl, 1.0, 0.0).astype(BF16)

    sm = sm_ref[...]
    lane = lax.broadcasted_iota(jnp.int32, (c, LANES), 1)
    beta = jnp.where((lane >= 64) & (lane < 96), jax.nn.sigmoid(sm), 0.0)
    g = jnp.where(lane >= 96, -jnp.exp(alog_ref[...]) * _softplus(sm + dtb_ref[...]), 0.0)
    gc = _dot_split_rhs(tril, g, 3)
    gc_hi, gc_lo = _two_term(gc)
    gc = gc_hi.astype(F32) + gc_lo.astype(F32)
    gcrep_ref[...] = _replicate_dot(gc, eg_ref[...])
    betarep_ref[...] = _replicate_dot(beta, eb_ref[...])
    xt_ref[...] = jnp.concatenate([gc, jnp.zeros_like(gc)], axis=0).T

    def heads_body(it, carry):
        nv = GDN_HEADS_PER_ITER
        vheads = [it * nv + e for e in range(nv)]
        chs = [pl.multiple_of(h * GDN_DK, GDN_DK) for h in vheads]
        qhs, khs, kkts, qkts = [], [], [], []
        for e in range(nv // rep):
            cq = pl.multiple_of((it * (nv // rep) + e) * GDN_DK, GDN_DK)
            qh = qn_ref[:, pl.ds(cq, GDN_DK)]
            kh = kn_ref[:, pl.ds(cq, GDN_DK)]
            kb = kh.astype(BF16)
            kkt = _dot_nt(kb, kb)
            qkt = _dot_nt(qh.astype(BF16), kb)
            for _ in range(rep):
                qhs.append(qh)
                khs.append(kh)
                kkts.append(kkt)
                qkts.append(qkt)
        gcols = [gcrep_ref[:, pl.ds(ch, GDN_DK)] for ch in chs]
        bcols = [betarep_ref[:, pl.ds(ch, GDN_DK)] for ch in chs]
        grows = [xt_ref[pl.ds(96 + h, 1), :][:, :c] for h in vheads]
        gams = [jnp.where(incl, jnp.exp(gcol[:, :c] - grow), 0.0) for gcol, grow in zip(gcols, grows)]
        ms = [jnp.where(strict, kkt * gam * bcol[:, :c], 0.0) for kkt, gam, bcol in zip(kkts, gams, bcols)]
        ns = _unit_lower_inverse_minus_eye(ms, row, col)
        egs = [jnp.exp(gcol) for gcol in gcols]
        rhss = [jnp.concatenate([vact_ref[:, pl.ds(ch, GDN_DK)] * bcol, kh * (bcol * eg)], axis=1)
                for ch, bcol, kh, eg in zip(chs, bcols, khs, egs)]
        sols = [rhs + _bdot(n, rhs) for n, rhs in zip(ns, rhss)]
        ss = [s_ref[h] for h in vheads]
        sbs = [s.astype(BF16) for s in ss]
        vbs = [(sol[:, :GDN_DK] - jnp.dot(sol[:, GDN_DK:].astype(BF16), sb, preferred_element_type=F32)).astype(BF16)
               for sol, sb in zip(sols, sbs)]
        os_ = [jnp.dot((qh * eg).astype(BF16), sb, preferred_element_type=F32)
               + jnp.dot((qkt * gam).astype(BF16), vb, preferred_element_type=F32)
               for qh, eg, sb, qkt, gam, vb in zip(qhs, egs, sbs, qkts, gams, vbs)]
        for h, s, gcol, kh, vb in zip(vheads, ss, gcols, khs, vbs):
            glast = gcol[c - 1:c, :]
            kdec = kh * jnp.exp(glast - gcol)
            s_ref[h] = s * jnp.exp(glast) + lax.dot_general(
                kdec.astype(BF16), vb, (((0,), (0,)), ((), ())), preferred_element_type=F32)
        for ch, o in zip(chs, os_):
            msq = jnp.mean(o * o, axis=-1, keepdims=True)
            z = z_ref[:, pl.ds(ch, GDN_DK)].astype(F32)
            o_ref[:, pl.ds(ch, GDN_DK)] = (o * lax.rsqrt(msq + NORM_EPS) * nw_ref[...] * _silu(z)).astype(o_ref.dtype)
        return carry

    lax.fori_loop(0, GDN_V_HEADS // GDN_HEADS_PER_ITER, heads_body, 0)


def _gdn(proj, small, conv_w, dt_bias, a_log, norm_w, bsz, tb):
    c = GDN_CHUNK
    nc = tb // c
    t = bsz * tb
    dtb = jnp.pad(dt_bias, (LANES - GDN_V_HEADS, 0)).reshape(1, LANES)
    alog = jnp.pad(a_log, (LANES - GDN_V_HEADS, 0)).reshape(1, LANES)
    r = jnp.arange(2 * LANES)[:, None] % LANES
    head_of_col = jnp.arange(D_V)[None, :] // GDN_DK
    eg = (r == 96 + head_of_col).astype(BF16)
    eb = (r == 64 + head_of_col).astype(BF16)
    const = lambda b, i: (0, 0)
    rowblk = lambda off: (lambda b, i: (b * nc + i, off))
    return pl.pallas_call(
        _gdn_kernel,
        grid=(bsz, nc),
        in_specs=[
            pl.BlockSpec((c, D_QK), rowblk(PROJ_Q // D_QK)),
            pl.BlockSpec((c, D_QK), rowblk(PROJ_K // D_QK)),
            pl.BlockSpec((c, D_V), rowblk(PROJ_V // D_V)),
            pl.BlockSpec((c, D_V), rowblk(PROJ_ZG // D_V)),
            pl.BlockSpec((c, LANES), rowblk(0)),
            pl.BlockSpec((GDN_CONV, D_QK), const),
            pl.BlockSpec((GDN_CONV, D_QK), const),
            pl.BlockSpec((GDN_CONV, D_V), const),
            pl.BlockSpec((1, LANES), const),
            pl.BlockSpec((1, LANES), const),
            pl.BlockSpec((1, GDN_DK), const),
            pl.BlockSpec((2 * LANES, D_V), const),
            pl.BlockSpec((2 * LANES, D_V), const),
        ],
        out_specs=pl.BlockSpec((c, D_V), rowblk(0)),
        out_shape=jax.ShapeDtypeStruct((t, D_V), BF16),
        scratch_shapes=[
            pltpu.VMEM((GDN_V_HEADS, GDN_DK, GDN_DK), F32),
            pltpu.VMEM((c + SUBLANES, D_QK), F32),
            pltpu.VMEM((c + SUBLANES, D_QK), F32),
            pltpu.VMEM((c + SUBLANES, D_V), F32),
            pltpu.VMEM((c, D_QK), F32),
            pltpu.VMEM((c, D_QK), F32),
            pltpu.VMEM((c, D_V), F32),
            pltpu.VMEM((c, D_V), F32),
            pltpu.VMEM((c, D_V), F32),
            pltpu.VMEM((LANES, LANES), F32),
        ],
        compiler_params=_cparams(("parallel", "arbitrary"), 48 << 20),
        name="gdn_scan",
    )(proj, proj, proj, proj, small, conv_w[:, :D_QK], conv_w[:, D_QK:2 * D_QK], conv_w[:, 2 * D_QK:],
      dtb, alog, norm_w.reshape(1, GDN_DK), eg, eb)


def _final_norm_kernel(a_ref, b_ref, g_ref, o_ref):
    x = jnp.concatenate([a_ref[N_META:, :], b_ref[...]], axis=0)
    ms = jnp.mean(x * x, axis=-1, keepdims=True)
    o_ref[...] = (x * lax.rsqrt(ms + NORM_EPS) * g_ref[...]).astype(o_ref.dtype)


def _final_norm(h, gain, bsz, seq, tb):
    d = h.shape[1]
    r = 128
    nb = seq // r
    return pl.pallas_call(
        _final_norm_kernel,
        grid=(bsz, nb),
        in_specs=[
            pl.BlockSpec((r, d), lambda b, i: (b * (tb // r) + i, 0)),
            pl.BlockSpec((N_META, d), lambda b, i: ((b * tb + (i + 1) * r) // N_META, 0)),
            pl.BlockSpec((1, d), lambda b, i: (0, 0)),
        ],
        out_specs=pl.BlockSpec((None, r, d), lambda b, i: (b, i, 0)),
        out_shape=jax.ShapeDtypeStruct((bsz, seq, d), F32),
        compiler_params=_cparams(("parallel", "parallel"), 4 * r * d * 4 * 2 + (8 << 20)),
        name="final_norm",
    )(h, h, gain.reshape(1, d))


def _rearranged_in_proj(w_in):
    o = 0
    parts = {}
    for name, width in (("zm", D_SSM), ("xs", D_SSM), ("bc", D_BC), ("dt", SSM_HEADS), ("q", D_QK), ("k", D_QK),
                        ("v", D_V), ("zg", D_V), ("b", GDN_V_HEADS), ("a", GDN_V_HEADS), ("gm", D_MODEL),
                        ("gg", D_MODEL)):
        parts[name] = w_in[:, o:o + width]
        o += width
    big = jnp.concatenate([parts[n] for n in ("zm", "xs", "zg", "v", "gm", "gg", "bc", "q", "k")], axis=1)
    small = jnp.concatenate([parts["dt"], parts["b"], parts["a"]], axis=1)
    return big.astype(BF16), small.astype(BF16)


def kernel(x, meta_tokens, ffn1_norm, ffn1_w_gate_up, ffn1_w_down, mix_norm, w_in, ssm_conv_w, ssm_conv_b,
           ssm_dt_bias, ssm_a_log, ssm_d, ssm_norm, ssm_w_out, gdn_conv_w, gdn_dt_bias, gdn_a_log, gdn_norm,
           gdn_w_out, gate_b, w_o, ffn2_norm, ffn2_w_gate_up, ffn2_w_down, final_norm):
    bsz, seq, d = x.shape
    ltot = seq + N_META
    tb = -(-ltot // SSM_CHUNK) * SSM_CHUNK
    depth = w_in.shape[0]
    h, hg, ssq = _embed_prep(x, meta_tokens, ffn1_norm[0], tb)
    for i in range(depth):
        a = _ffn_up(hg, ssq, ffn1_w_gate_up, i)
        h, hg, ssq = _mm_res(a, ffn1_w_down, i, h, mix_norm[i])
        w_big, w_small = _rearranged_in_proj(w_in[i])
        proj, small = _inproj(hg, ssq, w_big, w_small)
        y = _ssd(proj, small, ssm_conv_w[i], ssm_conv_b[i], ssm_dt_bias[i], ssm_a_log[i], ssm_d[i], ssm_norm[i],
                 bsz, tb)
        o = _gdn(proj, small, gdn_conv_w[i], gdn_dt_bias[i], gdn_a_log[i], gdn_norm[i], bsz, tb)
        merged = _branch_merge(y, o, ssm_w_out, gdn_w_out, i, proj, gate_b[i])
        h, hg, ssq = _mm_res(merged, w_o, i, h, ffn2_norm[i])
        a = _ffn_up(hg, ssq, ffn2_w_gate_up, i)
        if i + 1 < depth:
            h, hg, ssq = _mm_res(a, ffn2_w_down, i, h, ffn1_norm[i + 1])
        else:
            h = _mm_res(a, ffn2_w_down, i, h, None)
    return _final_norm(h, final_norm, bsz, seq, tb)
```

```python
import functools

import jax
import jax.numpy as jnp
from jax import lax
from jax.experimental import pallas as pl
from jax.experimental.pallas import tpu as pltpu

F32 = jnp.float32
BF16 = jnp.bfloat16

D_MODEL = 4096
N_META = 16
NORM_EPS = 1e-6
D_FF = 2 * D_MODEL
SSM_HEADS = 64
SSM_HEAD_DIM = 64
SSM_GROUPS = 8
SSM_STATE = 128
SSM_CONV = 4
SSM_CHUNK = 128
SSM_GROUPS_PER_ITER = 8
D_SSM = SSM_HEADS * SSM_HEAD_DIM
D_BC = 2 * SSM_GROUPS * SSM_STATE
GDN_DK = 128
GDN_QK_HEADS = 16
GDN_V_HEADS = 32
GDN_CONV = 4
GDN_CHUNK = 64
GDN_HEADS_PER_ITER = 32
D_QK = GDN_QK_HEADS * GDN_DK
D_V = GDN_V_HEADS * GDN_DK

LANES = 128
SUBLANES = 8
VMEM_CAP = 60 * 1024 * 1024

PA_ZM, PA_XS, PA_BC = 0, 4096, 8192
PB_Q, PB_K, PB_V, PB_ZG = 0, 2048, 4096, 8192
PC_GM, PC_GG = 0, 4096


def _cparams(sem, vmem_bytes):
    return pltpu.CompilerParams(dimension_semantics=sem, vmem_limit_bytes=min(int(vmem_bytes), VMEM_CAP))


def _pick_tile(n, candidates):
    for c in candidates:
        if n % c == 0:
            return c
    raise ValueError(f"no tile for {n} in {candidates}")


def _silu(x):
    return x * jax.nn.sigmoid(x)


def _softplus(x):
    return jnp.maximum(x, 0.0) + jnp.log(1.0 + jnp.exp(-jnp.abs(x)))


def _split_bf16(x, passes):
    parts = []
    r = x
    for p in range(passes):
        b = r.astype(BF16)
        parts.append(b)
        if p + 1 < passes:
            r = r - b.astype(F32)
    return parts


def _dot_split_rhs(a_bf16, x, passes):
    acc = None
    for p in _split_bf16(x, passes):
        d = jnp.dot(a_bf16, p, preferred_element_type=F32)
        acc = d if acc is None else acc + d
    return acc


def _two_term(x):
    hi = x.astype(BF16)
    return hi, (x - hi.astype(F32)).astype(BF16)


def _replicate_dot(x, e2_bf16):
    hi, lo = _two_term(x)
    return jnp.dot(jnp.concatenate([hi, lo], axis=1), e2_bf16, preferred_element_type=F32)


def _dot_nt(a, b):
    return lax.dot_general(a, b, (((1,), (1,)), ((), ())), preferred_element_type=F32)


def _bdot(a, b):
    return jnp.dot(a.astype(BF16), b.astype(BF16), preferred_element_type=F32)


def _lane_partial_sumsq(x):
    sq = x * x
    acc = sq[:, 0:LANES]
    for k in range(1, x.shape[1] // LANES):
        acc = acc + sq[:, k * LANES:(k + 1) * LANES]
    return acc


def _row_rscale(ssq):
    return lax.rsqrt(jnp.sum(ssq, axis=-1, keepdims=True) * (1.0 / D_MODEL) + NORM_EPS)


def _embed_prep_kernel(xa_ref, xb_ref, meta_ref, g_ref, h_ref, hg_ref, ssq_ref, *, nxb):
    r = pl.program_id(1)
    rows = h_ref.shape[0]
    top = jnp.where(r == 0, meta_ref[...], jnp.where(r <= nxb, xb_ref[...], 0.0))
    bot = jnp.where(r < nxb, xa_ref[0:rows - N_META, :], 0.0)
    hb = jnp.concatenate([top, bot], axis=0)
    h_ref[...] = hb
    hg_ref[...] = (hb * g_ref[...]).astype(hg_ref.dtype)
    ssq_ref[...] = _lane_partial_sumsq(hb)


def _embed_prep(x, meta_tokens, gain, tb):
    bsz, seq, d = x.shape
    rows = SSM_CHUNK
    nxb = seq // rows
    nrb = tb // rows
    per16 = rows // N_META
    t = bsz * tb
    return pl.pallas_call(
        functools.partial(_embed_prep_kernel, nxb=nxb),
        grid=(bsz, nrb),
        in_specs=[
            pl.BlockSpec((None, rows, d), lambda b, r: (b, jnp.minimum(r, nxb - 1), 0)),
            pl.BlockSpec((None, N_META, d), lambda b, r: (b, jnp.clip(r * per16 - 1, 0, seq // N_META - 1), 0)),
            pl.BlockSpec((N_META, d), lambda b, r: (0, 0)),
            pl.BlockSpec((1, d), lambda b, r: (0, 0)),
        ],
        out_specs=[
            pl.BlockSpec((rows, d), lambda b, r: (b * nrb + r, 0)),
            pl.BlockSpec((rows, d), lambda b, r: (b * nrb + r, 0)),
            pl.BlockSpec((rows, LANES), lambda b, r: (b * nrb + r, 0)),
        ],
        out_shape=[jax.ShapeDtypeStruct((t, d), F32), jax.ShapeDtypeStruct((t, d), BF16),
                   jax.ShapeDtypeStruct((t, LANES), F32)],
        compiler_params=_cparams(("parallel", "parallel"), 4 * rows * d * (4 + 4 + 4 + 2) + (8 << 20)),
        name="embed_prep",
    )(x, x, meta_tokens.astype(x.dtype), gain.reshape(1, d))


def _tile0_col(i, j):
    return jnp.where(i == 0, j, 0)


def _skip0_col(i, j):
    return jnp.where(i == 0, 0, j)


def _ffn_up_math(x, r, wg, wu):
    g = jnp.dot(x, wg, preferred_element_type=F32) * r
    u = jnp.dot(x, wu, preferred_element_type=F32) * r
    return (_silu(g) * u * 0.5).astype(BF16)


def _ffn_up_first_kernel(x_ref, ssq_ref, wg_ref, wu_ref, o_ref, wgb_ref, wub_ref):
    wg = wg_ref[...].astype(BF16)
    wu = wu_ref[...].astype(BF16)
    wgb_ref[...] = wg
    wub_ref[...] = wu
    o_ref[...] = _ffn_up_math(x_ref[...], _row_rscale(ssq_ref[...]), wg, wu)


def _ffn_up_rest_kernel(x_ref, ssq_ref, wg_ref, wu_ref, o0_ref, o_ref):
    i = pl.program_id(0)

    @pl.when(i == 0)
    def _():
        o_ref[...] = o0_ref[...]

    @pl.when(i > 0)
    def _():
        o_ref[...] = _ffn_up_math(x_ref[...], _row_rscale(ssq_ref[...]), wg_ref[...], wu_ref[...])


def _ffn_up(hg, ssq, w_gate_up, layer):
    t, d = hg.shape
    tm = _pick_tile(t, (1280, 640, 256, 128))
    tn1 = 256
    nj1 = D_FF // tn1
    vmem1 = 2 * (tm * d * 2 + tm * LANES * 4 + 2 * d * tn1 * (4 + 2) + tm * tn1 * 2) + 6 * tm * tn1 * 4 + (4 << 20)
    a0, wgb, wub = pl.pallas_call(
        _ffn_up_first_kernel,
        grid=(nj1,),
        in_specs=[
            pl.BlockSpec((tm, d), lambda j: (0, 0)),
            pl.BlockSpec((tm, LANES), lambda j: (0, 0)),
            pl.BlockSpec((None, d, tn1), lambda j: (layer, 0, j)),
            pl.BlockSpec((None, d, tn1), lambda j: (layer, 0, j + nj1)),
        ],
        out_specs=[
            pl.BlockSpec((tm, tn1), lambda j: (0, j)),
            pl.BlockSpec((d, tn1), lambda j: (0, j)),
            pl.BlockSpec((d, tn1), lambda j: (0, j)),
        ],
        out_shape=[jax.ShapeDtypeStruct((tm, D_FF), BF16), jax.ShapeDtypeStruct((d, D_FF), BF16),
                   jax.ShapeDtypeStruct((d, D_FF), BF16)],
        compiler_params=_cparams(("arbitrary",), vmem1),
        name="ffn_up_first",
    )(hg, ssq, w_gate_up, w_gate_up)
    if t == tm:
        return a0
    tn = 512
    vmem = 2 * (tm * d * 2 + tm * LANES * 4 + 2 * d * tn * 2 + 2 * tm * tn * 2) + 6 * tm * tn * 4 + (4 << 20)
    return pl.pallas_call(
        _ffn_up_rest_kernel,
        grid=(t // tm, D_FF // tn),
        in_specs=[
            pl.BlockSpec((tm, d), lambda i, j: (i, 0)),
            pl.BlockSpec((tm, LANES), lambda i, j: (i, 0)),
            pl.BlockSpec((d, tn), lambda i, j: (0, _skip0_col(i, j))),
            pl.BlockSpec((d, tn), lambda i, j: (0, _skip0_col(i, j))),
            pl.BlockSpec((tm, tn), lambda i, j: (0, _tile0_col(i, j))),
        ],
        out_specs=pl.BlockSpec((tm, tn), lambda i, j: (i, j)),
        out_shape=jax.ShapeDtypeStruct((t, D_FF), BF16),
        compiler_params=_cparams(("parallel", "arbitrary"), vmem),
        name="ffn_up",
    )(hg, ssq, wgb, wub, a0)


def _mm_res_store(hn, gn_ref, j, o_ref, hg_ref, ssq_ref):
    o_ref[...] = hn
    if hg_ref is None:
        return
    hg_ref[...] = (hn * gn_ref[...]).astype(hg_ref.dtype)
    part = _lane_partial_sumsq(hn)

    @pl.when(j == 0)
    def _():
        ssq_ref[...] = part

    @pl.when(j > 0)
    def _():
        ssq_ref[...] += part


def _mm_res_first_kernel(a_ref, w_ref, h_ref, gn_ref, o_ref, wb_ref, hg_ref=None, ssq_ref=None):
    w = w_ref[...].astype(BF16)
    wb_ref[...] = w
    hn = h_ref[...] + jnp.dot(a_ref[...], w, preferred_element_type=F32)
    _mm_res_store(hn, gn_ref, pl.program_id(0), o_ref, hg_ref, ssq_ref)


def _mm_res_rest_kernel(a_ref, w_ref, h_ref, gn_ref, o0_ref, hg0_ref, ssq0_ref, o_ref, hg_ref, ssq_ref):
    i = pl.program_id(0)

    @pl.when(i == 0)
    def _():
        o_ref[...] = o0_ref[...]
        if hg_ref is not None:
            hg_ref[...] = hg0_ref[...]
            ssq_ref[...] = ssq0_ref[...]

    @pl.when(i > 0)
    def _():
        hn = h_ref[...] + jnp.dot(a_ref[...], w_ref[...], preferred_element_type=F32)
        _mm_res_store(hn, gn_ref, pl.program_id(1), o_ref, hg_ref, ssq_ref)


def _mm_res_rest_plain_kernel(a_ref, w_ref, h_ref, gn_ref, o0_ref, o_ref):
    _mm_res_rest_kernel(a_ref, w_ref, h_ref, gn_ref, o0_ref, None, None, o_ref, None, None)


def _mm_res(a, w, layer, h, next_gain):
    t, kdim = a.shape
    n = w.shape[2]
    emit = next_gain is not None
    gn = (next_gain if emit else jnp.ones((n,), F32)).reshape(1, n)
    tm = _pick_tile(t, (640, 256, 128))
    tn1 = 256
    blk = lambda shape, imap: pl.BlockSpec(shape, imap)
    vmem1 = 2 * (tm * kdim * 2 + kdim * tn1 * (4 + 2) + tm * tn1 * (4 + 4 + 2) + tm * LANES * 4) + 4 * tm * tn1 * 4 + (4 << 20)
    out_specs1 = [blk((tm, tn1), lambda j: (0, j)), blk((kdim, tn1), lambda j: (0, j))]
    out_shape1 = [jax.ShapeDtypeStruct((tm, n), F32), jax.ShapeDtypeStruct((kdim, n), BF16)]
    if emit:
        out_specs1 += [blk((tm, tn1), lambda j: (0, j)), blk((tm, LANES), lambda j: (0, 0))]
        out_shape1 += [jax.ShapeDtypeStruct((tm, n), BF16), jax.ShapeDtypeStruct((tm, LANES), F32)]
    first = pl.pallas_call(
        _mm_res_first_kernel,
        grid=(n // tn1,),
        in_specs=[blk((tm, kdim), lambda j: (0, 0)), blk((None, kdim, tn1), lambda j: (layer, 0, j)),
                  blk((tm, tn1), lambda j: (0, j)), blk((1, tn1), lambda j: (0, j))],
        out_specs=out_specs1,
        out_shape=out_shape1,
        compiler_params=_cparams(("arbitrary",), vmem1),
        name="mm_res_first",
    )(a, w, h, gn)
    if t == tm:
        return (first[0], first[2], first[3]) if emit else first[0]
    tn = 512 if kdim > 4096 else 1024
    vmem = (2 * (tm * kdim * 2 + kdim * tn * 2 + tm * tn * (4 + 4 + 2 + 4 + 2) + 2 * tm * LANES * 4)
            + 3 * tm * tn * 4 + (4 << 20))
    in_specs = [blk((tm, kdim), lambda i, j: (i, 0)), blk((kdim, tn), lambda i, j: (0, _skip0_col(i, j))),
                blk((tm, tn), lambda i, j: (i, _skip0_col(i, j))), blk((1, tn), lambda i, j: (0, j)),
                blk((tm, tn), lambda i, j: (0, _tile0_col(i, j)))]
    out_specs = [blk((tm, tn), lambda i, j: (i, j))]
    out_shape = [jax.ShapeDtypeStruct((t, n), F32)]
    args = [a, first[1], h, gn, first[0]]
    if emit:
        in_specs += [blk((tm, tn), lambda i, j: (0, _tile0_col(i, j))), blk((tm, LANES), lambda i, j: (0, 0))]
        out_specs += [blk((tm, tn), lambda i, j: (i, j)), blk((tm, LANES), lambda i, j: (i, 0))]
        out_shape += [jax.ShapeDtypeStruct((t, n), BF16), jax.ShapeDtypeStruct((t, LANES), F32)]
        args += [first[2], first[3]]
    rest = pl.pallas_call(
        _mm_res_rest_kernel if emit else _mm_res_rest_plain_kernel,
        grid=(t // tm, n // tn),
        in_specs=in_specs,
        out_specs=out_specs,
        out_shape=out_shape,
        compiler_params=_cparams(("parallel", "arbitrary"), vmem),
        name="mm_res",
    )(*args)
    return tuple(rest) if emit else rest[0]


def _inproj_kernel(x_ref, ssq_ref, w_ref, o_ref):
    r = _row_rscale(ssq_ref[...])
    o_ref[...] = (jnp.dot(x_ref[...], w_ref[...], preferred_element_type=F32) * r).astype(o_ref.dtype)


def _inproj_small_kernel(x_ref, ssq_ref, w_ref, ws_ref, o_ref, os_ref):
    _inproj_kernel(x_ref, ssq_ref, w_ref, o_ref)

    @pl.when(pl.program_id(1) == 0)
    def _():
        os_ref[...] = jnp.dot(x_ref[...], ws_ref[...], preferred_element_type=F32) * _row_rscale(ssq_ref[...])


def _inproj(hg, ssq, w, w_small=None):
    t, d = hg.shape
    n = w.shape[1]
    tm = _pick_tile(t, (1280, 640, 256, 128))
    tn = 1024
    vmem = (2 * (tm * d * 2 + tm * LANES * 4 + d * tn * 2 + tm * tn * 2 + d * LANES * 2 + tm * LANES * 4)
            + 2 * tm * tn * 4 + (4 << 20))
    in_specs = [
        pl.BlockSpec((tm, d), lambda i, j: (i, 0)),
        pl.BlockSpec((tm, LANES), lambda i, j: (i, 0)),
        pl.BlockSpec((d, tn), lambda i, j: (0, j)),
    ]
    out_specs = [pl.BlockSpec((tm, tn), lambda i, j: (i, j))]
    out_shape = [jax.ShapeDtypeStruct((t, n), BF16)]
    args = [hg, ssq, w]
    if w_small is not None:
        in_specs.append(pl.BlockSpec((d, LANES), lambda i, j: (0, 0)))
        out_specs.append(pl.BlockSpec((tm, LANES), lambda i, j: (i, 0)))
        out_shape.append(jax.ShapeDtypeStruct((t, LANES), F32))
        args.append(w_small)
    out = pl.pallas_call(
        _inproj_kernel if w_small is None else _inproj_small_kernel,
        grid=(t // tm, n // tn),
        in_specs=in_specs,
        out_specs=out_specs,
        out_shape=out_shape,
        compiler_params=_cparams(("parallel", "arbitrary"), vmem),
        name="inproj",
    )(*args)
    return out[0] if w_small is None else tuple(out)


def _branch_math(y, o, wm, wg, gm, gg, bm, bg):
    br_m = jnp.dot(y, wm, preferred_element_type=F32)
    br_g = jnp.dot(o, wg, preferred_element_type=F32)
    gate_m = jax.nn.sigmoid(gm.astype(F32) + bm)
    gate_g = jax.nn.sigmoid(gg.astype(F32) + bg)
    return (gate_m * br_m + gate_g * br_g).astype(BF16)


def _branch_first_kernel(y_ref, o_ref, wm_ref, wg_ref, gm_ref, gg_ref, bm_ref, bg_ref, out_ref, wmb_ref, wgb_ref):
    wm = wm_ref[...].astype(BF16)
    wg = wg_ref[...].astype(BF16)
    wmb_ref[...] = wm
    wgb_ref[...] = wg
    out_ref[...] = _branch_math(y_ref[...], o_ref[...], wm, wg, gm_ref[...], gg_ref[...], bm_ref[...], bg_ref[...])


def _branch_rest_kernel(y_ref, o_ref, wm_ref, wg_ref, gm_ref, gg_ref, bm_ref, bg_ref, out0_ref, out_ref):
    i = pl.program_id(0)

    @pl.when(i == 0)
    def _():
        out_ref[...] = out0_ref[...]

    @pl.when(i > 0)
    def _():
        out_ref[...] = _branch_math(y_ref[...], o_ref[...], wm_ref[...], wg_ref[...], gm_ref[...], gg_ref[...],
                                    bm_ref[...], bg_ref[...])


def _branch_merge(y, o, w_m, w_g, layer, proj, gate_b):
    t, d = y.shape
    tm = _pick_tile(t, (640, 256, 128))
    gb = gate_b.reshape(1, 2 * d)
    tn1 = 256
    nj1 = d // tn1
    vmem1 = 2 * (2 * tm * d * 2 + 2 * d * tn1 * (4 + 2) + 3 * tm * tn1 * 2) + 6 * tm * tn1 * 4 + (4 << 20)
    out0, wmb, wgb = pl.pallas_call(
        _branch_first_kernel,
        grid=(nj1,),
        in_specs=[
            pl.BlockSpec((tm, d), lambda j: (0, 0)),
            pl.BlockSpec((tm, d), lambda j: (0, 0)),
            pl.BlockSpec((None, d, tn1), lambda j: (layer, 0, j)),
            pl.BlockSpec((None, d, tn1), lambda j: (layer, 0, j)),
            pl.BlockSpec((tm, tn1), lambda j: (0, j + PC_GM // tn1)),
            pl.BlockSpec((tm, tn1), lambda j: (0, j + PC_GG // tn1)),
            pl.BlockSpec((1, tn1), lambda j: (0, j)),
            pl.BlockSpec((1, tn1), lambda j: (0, j + nj1)),
        ],
        out_specs=[
            pl.BlockSpec((tm, tn1), lambda j: (0, j)),
            pl.BlockSpec((d, tn1), lambda j: (0, j)),
            pl.BlockSpec((d, tn1), lambda j: (0, j)),
        ],
        out_shape=[jax.ShapeDtypeStruct((tm, d), BF16), jax.ShapeDtypeStruct((d, d), BF16),
                   jax.ShapeDtypeStruct((d, d), BF16)],
        compiler_params=_cparams(("arbitrary",), vmem1),
        name="branch_merge_first",
    )(y, o, w_m, w_g, proj, proj, gb, gb)
    if t == tm:
        return out0
    tn = 512
    jm, jg = PC_GM // tn, PC_GG // tn
    nj = d // tn
    vmem = 2 * (2 * tm * d * 2 + 2 * d * tn * 2 + 4 * tm * tn * 2) + 6 * tm * tn * 4 + (4 << 20)
    return pl.pallas_call(
        _branch_rest_kernel,
        grid=(t // tm, nj),
        in_specs=[
            pl.BlockSpec((tm, d), lambda i, j: (i, 0)),
            pl.BlockSpec((tm, d), lambda i, j: (i, 0)),
            pl.BlockSpec((d, tn), lambda i, j: (0, _skip0_col(i, j))),
            pl.BlockSpec((d, tn), lambda i, j: (0, _skip0_col(i, j))),
            pl.BlockSpec((tm, tn), lambda i, j: (i, _skip0_col(i, j) + jm)),
            pl.BlockSpec((tm, tn), lambda i, j: (i, _skip0_col(i, j) + jg)),
            pl.BlockSpec((1, tn), lambda i, j: (0, j)),
            pl.BlockSpec((1, tn), lambda i, j: (0, j + nj)),
            pl.BlockSpec((tm, tn), lambda i, j: (0, _tile0_col(i, j))),
        ],
        out_specs=pl.BlockSpec((tm, tn), lambda i, j: (i, j)),
        out_shape=jax.ShapeDtypeStruct((t, d), BF16),
        compiler_params=_cparams(("parallel", "arbitrary"), vmem),
        name="branch_merge",
    )(y, o, wmb, wgb, proj, proj, gb, gb, out0)


def _conv_silu(x_ref, pad_ref, w_ref, b_ref, rows):
    pad_ref[SUBLANES:SUBLANES + rows, :] = x_ref[...].astype(F32)
    acc = w_ref[3:4, :] * pad_ref[SUBLANES:SUBLANES + rows, :]
    for k in range(3):
        off = SUBLANES - 3 + k
        acc = acc + w_ref[k:k + 1, :] * pad_ref[off:off + rows, :]
    if b_ref is not None:
        acc = acc + b_ref[...]
    pad_ref[0:SUBLANES, :] = pad_ref[rows:rows + SUBLANES, :]
    return _silu(acc)


def _ssd_kernel(xs_ref, bc_ref, z_ref, sm_ref, cwx_ref, cbx_ref, cwbc_ref, cbbc_ref, dtb_ref, alog_ref,
                drep_ref, nw_ref, e64_ref, e128_ref,
                y_ref,
                st_ref, xpad_ref, bcpad_ref, xact_ref, xbf_ref, xw_ref, bcact_ref, larep_ref, exprep_ref,
                lat_ref, dtt_ref, yacc_ref):
    q = SSM_CHUNK
    hg = SSM_HEADS // SSM_GROUPS
    gw = hg * SSM_HEAD_DIM

    @pl.when(pl.program_id(1) == 0)
    def _():
        st_ref[...] = jnp.zeros_like(st_ref)
        xpad_ref[0:SUBLANES, :] = jnp.zeros((SUBLANES, D_SSM), F32)
        bcpad_ref[0:SUBLANES, :] = jnp.zeros((SUBLANES, D_BC), F32)

    xact = _conv_silu(xs_ref, xpad_ref, cwx_ref, cbx_ref, q)
    xact_ref[...] = xact
    xbf_ref[...] = xact.astype(BF16)
    bcact_ref[...] = _conv_silu(bc_ref, bcpad_ref, cwbc_ref, cbbc_ref, q)

    row = lax.broadcasted_iota(jnp.int32, (q, q), 0)
    col = lax.broadcasted_iota(jnp.int32, (q, q), 1)
    causal = row >= col
    tril = jnp.where(causal, 1.0, 0.0).astype(BF16)

    head_lane = col < SSM_HEADS
    dt = jnp.where(head_lane, _softplus(sm_ref[...] + dtb_ref[...]), 0.0)
    a = -jnp.exp(alog_ref[...])
    la = _dot_split_rhs(tril, dt * a, 3)
    la_hi, la_lo = _two_term(la)
    la = la_hi.astype(F32) + la_lo.astype(F32)
    la_last = la[q - 1:q, :]
    larep_ref[...] = _replicate_dot(la, e128_ref[...])
    exprep_ref[...] = _replicate_dot(jnp.exp(la), e64_ref[...])
    to_end = jnp.exp(la_last - la) * dt
    xw_ref[...] = (xact * _replicate_dot(to_end, e64_ref[...])).astype(BF16)
    lat_ref[...] = la.T
    dtt_ref[...] = dt.T

    def one_group(g):
        c0 = pl.multiple_of(g * SSM_STATE, SSM_STATE)
        x0 = pl.multiple_of(g * gw, gw)
        bg = bcact_ref[:, pl.ds(c0, SSM_STATE)]
        cg = bcact_ref[:, pl.ds(SSM_GROUPS * SSM_STATE + c0, SSM_STATE)].astype(BF16)
        cb = _dot_nt(cg, bg.astype(BF16))
        st_g = st_ref[:, pl.ds(x0, gw)]
        dec_g = exprep_ref[:, pl.ds(x0, gw)]
        y_inter = jnp.dot(cg, st_g.astype(BF16), preferred_element_type=F32) * dec_g
        st_ref[:, pl.ds(x0, gw)] = st_g * dec_g[q - 1:q, :] + jnp.dot(
            bg.T.astype(BF16), xw_ref[:, pl.ds(x0, gw)], preferred_element_type=F32)
        for pp in range(hg // 2):
            xp0 = pl.multiple_of(x0 + pp * LANES, LANES)
            xpair = xbf_ref[:, pl.ds(xp0, LANES)]
            res = []
            for e in range(2):
                h = g * hg + pp * 2 + e
                seg = larep_ref[:, pl.ds(pl.multiple_of(h * q, q), q)] - lat_ref[pl.ds(h, 1), :]
                dec = jnp.where(causal, jnp.exp(seg), 0.0)
                lmat = (cb * dec * dtt_ref[pl.ds(h, 1), :]).astype(BF16)
                res.append(jnp.dot(lmat, xpair, preferred_element_type=F32))
            y_intra = jnp.where(col < SSM_HEAD_DIM, res[0], res[1])
            yacc_ref[:, pl.ds(xp0, LANES)] = y_intra + y_inter[:, pp * LANES:(pp + 1) * LANES]

    def groups_body(it, carry):
        for e in range(SSM_GROUPS_PER_ITER):
            one_group(it * SSM_GROUPS_PER_ITER + e)
        return carry

    lax.fori_loop(0, SSM_GROUPS // SSM_GROUPS_PER_ITER, groups_body, 0)

    z = z_ref[...].astype(F32)
    y = (yacc_ref[...] + drep_ref[...] * xact_ref[...]) * _silu(z)
    for g in range(SSM_GROUPS):
        yg = y[:, g * gw:(g + 1) * gw]
        ms = jnp.mean(yg * yg, axis=-1, keepdims=True)
        y_ref[:, g * gw:(g + 1) * gw] = (yg * lax.rsqrt(ms + NORM_EPS) * nw_ref[:, g * gw:(g + 1) * gw]).astype(y_ref.dtype)


def _ssd(proj, small, conv_w, conv_b, dt_bias, a_log, d_skip, norm_w, bsz, tb):
    q = SSM_CHUNK
    nc = tb // q
    t = bsz * tb
    pad = LANES - SSM_HEADS
    dtb = jnp.pad(dt_bias, (0, pad)).reshape(1, LANES)
    alog = jnp.pad(a_log, (0, pad)).reshape(1, LANES)
    drep = jnp.repeat(d_skip, SSM_HEAD_DIM).reshape(1, D_SSM)
    r = jnp.arange(2 * LANES)[:, None] % LANES
    e64 = (jnp.arange(D_SSM)[None, :] // SSM_HEAD_DIM == r).astype(BF16)
    e128 = (jnp.arange(SSM_HEADS * q)[None, :] // q == r).astype(BF16)
    const = lambda b, c: (0, 0)
    rowblk = lambda off: (lambda b, c: (b * nc + c, off))
    return pl.pallas_call(
        _ssd_kernel,
        grid=(bsz, nc),
        in_specs=[
            pl.BlockSpec((q, D_SSM), rowblk(PA_XS // D_SSM)),
            pl.BlockSpec((q, D_BC), rowblk(PA_BC // D_BC)),
            pl.BlockSpec((q, D_SSM), rowblk(PA_ZM // D_SSM)),
            pl.BlockSpec((q, LANES), rowblk(0)),
            pl.BlockSpec((SSM_CONV, D_SSM), const),
            pl.BlockSpec((1, D_SSM), const),
            pl.BlockSpec((SSM_CONV, D_BC), const),
            pl.BlockSpec((1, D_BC), const),
            pl.BlockSpec((1, LANES), const),
            pl.BlockSpec((1, LANES), const),
            pl.BlockSpec((1, D_SSM), const),
            pl.BlockSpec((1, D_SSM), const),
            pl.BlockSpec((2 * LANES, D_SSM), const),
            pl.BlockSpec((2 * LANES, SSM_HEADS * q), const),
        ],
        out_specs=pl.BlockSpec((q, D_SSM), rowblk(0)),
        out_shape=jax.ShapeDtypeStruct((t, D_SSM), BF16),
        scratch_shapes=[
            pltpu.VMEM((SSM_STATE, D_SSM), F32),
            pltpu.VMEM((q + SUBLANES, D_SSM), F32),
            pltpu.VMEM((q + SUBLANES, D_BC), F32),
            pltpu.VMEM((q, D_SSM), F32),
            pltpu.VMEM((q, D_SSM), BF16),
            pltpu.VMEM((q, D_SSM), BF16),
            pltpu.VMEM((q, D_BC), F32),
            pltpu.VMEM((q, SSM_HEADS * q), F32),
            pltpu.VMEM((q, D_SSM), F32),
            pltpu.VMEM((LANES, q), F32),
            pltpu.VMEM((LANES, q), F32),
            pltpu.VMEM((q, D_SSM), F32),
        ],
        compiler_params=_cparams(("parallel", "arbitrary"), 56 << 20),
        name="ssd_scan",
    )(proj, proj, proj, small, conv_w[:, :D_SSM], conv_b[:D_SSM].reshape(1, D_SSM), conv_w[:, D_SSM:],
      conv_b[D_SSM:].reshape(1, D_BC), dtb, alog, drep, norm_w.reshape(1, D_SSM), e64, e128)


def _unit_lower_inverse_minus_eye(ms, row, col):
    blk16 = (row ^ col) < 16
    blk32 = (row ^ col) < 32
    ps = [jnp.where(blk16, m, 0.0) for m in ms]
    ns = [-p for p in ps]
    for _ in range(3):
        ps = [_bdot(p, p) for p in ps]
        ns = [n + p + _bdot(n, p) for n, p in zip(ns, ps)]
    for sel in (blk32 & ~blk16, ~blk32):
        offs = [jnp.where(sel, m, 0.0) for m in ms]
        us = [off + _bdot(n, off) for n, off in zip(ns, offs)]
        ns = [n - (u + _bdot(u, n)) for n, u in zip(ns, us)]
    return ns


def _gdn_kernel(q_ref, k_ref, v_ref, z_ref, sm_ref, cwq_ref, cwk_ref, cwv_ref, dtb_ref, alog_ref, nw_ref,
                eg_ref, eb_ref,
                o_ref,
                s_ref, qpad_ref, kpad_ref, vpad_ref, qn_ref, kn_ref, vact_ref, gcrep_ref, betarep_ref, xt_ref):
    c = GDN_CHUNK
    rep = GDN_V_HEADS // GDN_QK_HEADS

    @pl.when(pl.program_id(1) == 0)
    def _():
        s_ref[...] = jnp.zeros_like(s_ref)
        qpad_ref[0:SUBLANES, :] = jnp.zeros((SUBLANES, D_QK), F32)
        kpad_ref[0:SUBLANES, :] = jnp.zeros((SUBLANES, D_QK), F32)
        vpad_ref[0:SUBLANES, :] = jnp.zeros((SUBLANES, D_V), F32)

    qact = _conv_silu(q_ref, qpad_ref, cwq_ref, None, c)
    kact = _conv_silu(k_ref, kpad_ref, cwk_ref, None, c)
    vact_ref[...] = _conv_silu(v_ref, vpad_ref, cwv_ref, None, c)
    for h in range(GDN_QK_HEADS):
        sl = slice(h * GDN_DK, (h + 1) * GDN_DK)
        qh = qact[:, sl]
        kh = kact[:, sl]
        qn_ref[:, sl] = qh * lax.rsqrt(jnp.sum(qh * qh, axis=-1, keepdims=True) + NORM_EPS) * (GDN_DK ** -0.5)
        kn_ref[:, sl] = kh * lax.rsqrt(jnp.sum(kh * kh, axis=-1, keepdims=True) + NORM_EPS)

    row = lax.broadcasted_iota(jnp.int32, (c, c), 0)
    col = lax.broadcasted_iota(jnp.int32, (c, c), 1)
    incl = row >= col
    strict = row > col
    tril = jnp.where(incl, 1.0, 0.0).astype(BF16)

    sm = sm_ref[...]
    lane = lax.broadcasted_iota(jnp.int32, (c, LANES), 1)
    beta = jnp.where((lane >= 64) & (lane < 96), jax.nn.sigmoid(sm), 0.0)
    g = jnp.where(lane >= 96, -jnp.exp(alog_ref[...]) * _softplus(sm + dtb_ref[...]), 0.0)
    gc = _dot_split_rhs(tril, g, 3)
    gc_hi, gc_lo = _two_term(gc)
    gc = gc_hi.astype(F32) + gc_lo.astype(F32)
    gcrep_ref[...] = _replicate_dot(gc, eg_ref[...])
    betarep_ref[...] = _replicate_dot(beta, eb_ref[...])
    xt_ref[...] = jnp.concatenate([gc, jnp.zeros_like(gc)], axis=0).T

    def heads_body(it, carry):
        nv = GDN_HEADS_PER_ITER
        vheads = [it * nv + e for e in range(nv)]
        chs = [pl.multiple_of(h * GDN_DK, GDN_DK) for h in vheads]
        qhs, khs, kkts, qkts = [], [], [], []
        for e in range(nv // rep):
            cq = pl.multiple_of((it * (nv // rep) + e) * GDN_DK, GDN_DK)
            qh = qn_ref[:, pl.ds(cq, GDN_DK)]
            kh = kn_ref[:, pl.ds(cq, GDN_DK)]
            kb = kh.astype(BF16)
            kkt = _dot_nt(kb, kb)
            qkt = _dot_nt(qh.astype(BF16), kb)
            for _ in range(rep):
                qhs.append(qh)
                khs.append(kh)
                kkts.append(kkt)
                qkts.append(qkt)
        gcols = [gcrep_ref[:, pl.ds(ch, GDN_DK)] for ch in chs]
        bcols = [betarep_ref[:, pl.ds(ch, GDN_DK)] for ch in chs]
        grows = [xt_ref[pl.ds(96 + h, 1), :][:, :c] for h in vheads]
        gams = [jnp.where(incl, jnp.exp(gcol[:, :c] - grow), 0.0) for gcol, grow in zip(gcols, grows)]
        ms = [jnp.where(strict, kkt * gam * bcol[:, :c], 0.0) for kkt, gam, bcol in zip(kkts, gams, bcols)]
        ns = _unit_lower_inverse_minus_eye(ms, row, col)
        egs = [jnp.exp(gcol) for gcol in gcols]
        rhss = [jnp.concatenate([vact_ref[:, pl.ds(ch, GDN_DK)] * bcol, kh * (bcol * eg)], axis=1)
                for ch, bcol, kh, eg in zip(chs, bcols, khs, egs)]
        sols = [rhs + _bdot(n, rhs) for n, rhs in zip(ns, rhss)]
        ss = [s_ref[h] for h in vheads]
        sbs = [s.astype(BF16) for s in ss]
        vbs = [(sol[:, :GDN_DK] - jnp.dot(sol[:, GDN_DK:].astype(BF16), sb, preferred_element_type=F32)).astype(BF16)
               for sol, sb in zip(sols, sbs)]
        os_ = [jnp.dot((qh * eg).astype(BF16), sb, preferred_element_type=F32)
               + jnp.dot((qkt * gam).astype(BF16), vb, preferred_element_type=F32)
               for qh, eg, sb, qkt, gam, vb in zip(qhs, egs, sbs, qkts, gams, vbs)]
        for h, s, gcol, kh, vb in zip(vheads, ss, gcols, khs, vbs):
            glast = gcol[c - 1:c, :]
            kdec = kh * jnp.exp(glast - gcol)
            s_ref[h] = s * jnp.exp(glast) + lax.dot_general(
                kdec.astype(BF16), vb, (((0,), (0,)), ((), ())), preferred_element_type=F32)
        for ch, o in zip(chs, os_):
            msq = jnp.mean(o * o, axis=-1, keepdims=True)
            z = z_ref[:, pl.ds(ch, GDN_DK)].astype(F32)
            o_ref[:, pl.ds(ch, GDN_DK)] = (o * lax.rsqrt(msq + NORM_EPS) * nw_ref[...] * _silu(z)).astype(o_ref.dtype)
        return carry

    lax.fori_loop(0, GDN_V_HEADS // GDN_HEADS_PER_ITER, heads_body, 0)


def _gdn(proj, small, conv_w, dt_bias, a_log, norm_w, bsz, tb):
    c = GDN_CHUNK
    nc = tb // c
    t = bsz * tb
    dtb = jnp.pad(dt_bias, (LANES - GDN_V_HEADS, 0)).reshape(1, LANES)
    alog = jnp.pad(a_log, (LANES - GDN_V_HEADS, 0)).reshape(1, LANES)
    r = jnp.arange(2 * LANES)[:, None] % LANES
    head_of_col = jnp.arange(D_V)[None, :] // GDN_DK
    eg = (r == 96 + head_of_col).astype(BF16)
    eb = (r == 64 + head_of_col).astype(BF16)
    const = lambda b, i: (0, 0)
    rowblk = lambda off: (lambda b, i: (b * nc + i, off))
    return pl.pallas_call(
        _gdn_kernel,
        grid=(bsz, nc),
        in_specs=[
            pl.BlockSpec((c, D_QK), rowblk(PB_Q // D_QK)),
            pl.BlockSpec((c, D_QK), rowblk(PB_K // D_QK)),
            pl.BlockSpec((c, D_V), rowblk(PB_V // D_V)),
            pl.BlockSpec((c, D_V), rowblk(PB_ZG // D_V)),
            pl.BlockSpec((c, LANES), rowblk(0)),
            pl.BlockSpec((GDN_CONV, D_QK), const),
            pl.BlockSpec((GDN_CONV, D_QK), const),
            pl.BlockSpec((GDN_CONV, D_V), const),
            pl.BlockSpec((1, LANES), const),
            pl.BlockSpec((1, LANES), const),
            pl.BlockSpec((1, GDN_DK), const),
            pl.BlockSpec((2 * LANES, D_V), const),
            pl.BlockSpec((2 * LANES, D_V), const),
        ],
        out_specs=pl.BlockSpec((c, D_V), rowblk(0)),
        out_shape=jax.ShapeDtypeStruct((t, D_V), BF16),
        scratch_shapes=[
            pltpu.VMEM((GDN_V_HEADS, GDN_DK, GDN_DK), F32),
            pltpu.VMEM((c + SUBLANES, D_QK), F32),
            pltpu.VMEM((c + SUBLANES, D_QK), F32),
            pltpu.VMEM((c + SUBLANES, D_V), F32),
            pltpu.VMEM((c, D_QK), F32),
            pltpu.VMEM((c, D_QK), F32),
            pltpu.VMEM((c, D_V), F32),
            pltpu.VMEM((c, D_V), F32),
            pltpu.VMEM((c, D_V), F32),
            pltpu.VMEM((LANES, LANES), F32),
        ],
        compiler_params=_cparams(("parallel", "arbitrary"), 48 << 20),
        name="gdn_scan",
    )(proj, proj, proj, proj, small, conv_w[:, :D_QK], conv_w[:, D_QK:2 * D_QK], conv_w[:, 2 * D_QK:],
      dtb, alog, norm_w.reshape(1, GDN_DK), eg, eb)


def _final_norm_kernel(a_ref, b_ref, g_ref, o_ref):
    x = jnp.concatenate([a_ref[N_META:, :], b_ref[...]], axis=0)
    ms = jnp.mean(x * x, axis=-1, keepdims=True)
    o_ref[...] = (x * lax.rsqrt(ms + NORM_EPS) * g_ref[...]).astype(o_ref.dtype)


def _final_norm(h, gain, bsz, seq, tb):
    d = h.shape[1]
    r = 128
    nb = seq // r
    return pl.pallas_call(
        _final_norm_kernel,
        grid=(bsz, nb),
        in_specs=[
            pl.BlockSpec((r, d), lambda b, i: (b * (tb // r) + i, 0)),
            pl.BlockSpec((N_META, d), lambda b, i: ((b * tb + (i + 1) * r) // N_META, 0)),
            pl.BlockSpec((1, d), lambda b, i: (0, 0)),
        ],
        out_specs=pl.BlockSpec((None, r, d), lambda b, i: (b, i, 0)),
        out_shape=jax.ShapeDtypeStruct((bsz, seq, d), F32),
        compiler_params=_cparams(("parallel", "parallel"), 4 * r * d * 4 * 2 + (8 << 20)),
        name="final_norm",
    )(h, h, gain.reshape(1, d))


def _split_in_proj(w_in):
    a_end = 2 * D_SSM + D_BC
    b_start = a_end + SSM_HEADS
    b_end = b_start + 2 * D_QK + 2 * D_V
    c_start = b_end + 2 * GDN_V_HEADS
    small = jnp.concatenate([w_in[:, a_end:b_start], w_in[:, b_end:c_start]], axis=1)
    return (w_in[:, :a_end].astype(BF16), w_in[:, b_start:b_end].astype(BF16), w_in[:, c_start:].astype(BF16),
            small.astype(BF16))


def kernel(x, meta_tokens, ffn1_norm, ffn1_w_gate_up, ffn1_w_down, mix_norm, w_in, ssm_conv_w, ssm_conv_b,
           ssm_dt_bias, ssm_a_log, ssm_d, ssm_norm, ssm_w_out, gdn_conv_w, gdn_dt_bias, gdn_a_log, gdn_norm,
           gdn_w_out, gate_b, w_o, ffn2_norm, ffn2_w_gate_up, ffn2_w_down, final_norm):
    bsz, seq, d = x.shape
    ltot = seq + N_META
    tb = -(-ltot // SSM_CHUNK) * SSM_CHUNK
    depth = w_in.shape[0]
    h, hg, ssq = _embed_prep(x, meta_tokens, ffn1_norm[0], tb)
    for i in range(depth):
        a = _ffn_up(hg, ssq, ffn1_w_gate_up, i)
        h, hg, ssq = _mm_res(a, ffn1_w_down, i, h, mix_norm[i])
        w_a, w_b, w_c, w_small = _split_in_proj(w_in[i])
        proj_a, small = _inproj(hg, ssq, w_a, w_small)
        proj_b = _inproj(hg, ssq, w_b)
        proj_c = _inproj(hg, ssq, w_c)
        y = _ssd(proj_a, small, ssm_conv_w[i], ssm_conv_b[i], ssm_dt_bias[i], ssm_a_log[i], ssm_d[i], ssm_norm[i],
                 bsz, tb)
        o = _gdn(proj_b, small, gdn_conv_w[i], gdn_dt_bias[i], gdn_a_log[i], gdn_norm[i], bsz, tb)
        merged = _branch_merge(y, o, ssm_w_out, gdn_w_out, i, proj_c, gate_b[i])
        h, hg, ssq = _mm_res(merged, w_o, i, h, ffn2_norm[i])
        a = _ffn_up(hg, ssq, ffn2_w_gate_up, i)
        if i + 1 < depth:
            h, hg, ssq = _mm_res(a, ffn2_w_down, i, h, ffn1_norm[i + 1])
        else:
            h = _mm_res(a, ffn2_w_down, i, h, None)
    return _final_norm(h, final_norm, bsz, seq, tb)
```

```python
import functools

import jax
import jax.numpy as jnp
from jax import lax
from jax.experimental import pallas as pl
from jax.experimental.pallas import tpu as pltpu

F32 = jnp.float32
BF16 = jnp.bfloat16

D_MODEL = 4096
N_META = 16
NORM_EPS = 1e-6
D_FF = 2 * D_MODEL
SSM_HEADS = 64
SSM_HEAD_DIM = 64
SSM_GROUPS = 8
SSM_STATE = 128
SSM_CONV = 4
SSM_CHUNK = 128
SSM_GROUPS_PER_ITER = 8
D_SSM = SSM_HEADS * SSM_HEAD_DIM
D_BC = 2 * SSM_GROUPS * SSM_STATE
GDN_DK = 128
GDN_QK_HEADS = 16
GDN_V_HEADS = 32
GDN_CONV = 4
GDN_CHUNK = 64
GDN_HEADS_PER_ITER = 32
D_QK = GDN_QK_HEADS * GDN_DK
D_V = GDN_V_HEADS * GDN_DK

LANES = 128
SUBLANES = 8
VMEM_CAP = 60 * 1024 * 1024

PA_ZM, PA_XS, PA_BC = 0, 4096, 8192
PB_Q, PB_K, PB_V, PB_ZG = 0, 2048, 4096, 8192
PC_GM, PC_GG = 0, 4096


def _cparams(sem, vmem_bytes):
    return pltpu.CompilerParams(dimension_semantics=sem, vmem_limit_bytes=min(int(vmem_bytes), VMEM_CAP))


def _pick_tile(n, candidates):
    for c in candidates:
        if n % c == 0:
            return c
    raise ValueError(f"no tile for {n} in {candidates}")


def _silu(x):
    h = 0.5 * x
    return h + h * jnp.tanh(h)


def _softplus(x):
    return jnp.maximum(x, 0.0) + jnp.log(1.0 + jnp.exp(-jnp.abs(x)))


def _split_bf16(x, passes):
    parts = []
    r = x
    for p in range(passes):
        b = r.astype(BF16)
        parts.append(b)
        if p + 1 < passes:
            r = r - b.astype(F32)
    return parts


def _dot_split_rhs(a_bf16, x, passes):
    acc = None
    for p in _split_bf16(x, passes):
        d = jnp.dot(a_bf16, p, preferred_element_type=F32)
        acc = d if acc is None else acc + d
    return acc


def _two_term(x):
    hi = x.astype(BF16)
    return hi, (x - hi.astype(F32)).astype(BF16)


def _replicate_dot(x, e2_bf16):
    hi, lo = _two_term(x)
    return jnp.dot(jnp.concatenate([hi, lo], axis=1), e2_bf16, preferred_element_type=F32)


def _dot_nt(a, b):
    return lax.dot_general(a, b, (((1,), (1,)), ((), ())), preferred_element_type=F32)


def _bdot(a, b):
    return jnp.dot(a.astype(BF16), b.astype(BF16), preferred_element_type=F32)


def _lane_partial_sumsq(x):
    sq = x * x
    acc = sq[:, 0:LANES]
    for k in range(1, x.shape[1] // LANES):
        acc = acc + sq[:, k * LANES:(k + 1) * LANES]
    return acc


def _row_rscale(ssq):
    return lax.rsqrt(jnp.sum(ssq, axis=-1, keepdims=True) * (1.0 / D_MODEL) + NORM_EPS)


def _embed_prep_kernel(xa_ref, xb_ref, meta_ref, g_ref, h_ref, hg_ref, ssq_ref, *, nxb):
    r = pl.program_id(1)
    rows = h_ref.shape[0]
    top = jnp.where(r == 0, meta_ref[...], jnp.where(r <= nxb, xb_ref[...], 0.0))
    bot = jnp.where(r < nxb, xa_ref[0:rows - N_META, :], 0.0)
    hb = jnp.concatenate([top, bot], axis=0)
    h_ref[...] = hb
    hg_ref[...] = (hb * g_ref[...]).astype(hg_ref.dtype)
    ssq_ref[...] = _lane_partial_sumsq(hb)


def _embed_prep(x, meta_tokens, gain, tb):
    bsz, seq, d = x.shape
    rows = SSM_CHUNK
    nxb = seq // rows
    nrb = tb // rows
    per16 = rows // N_META
    t = bsz * tb
    return pl.pallas_call(
        functools.partial(_embed_prep_kernel, nxb=nxb),
        grid=(bsz, nrb),
        in_specs=[
            pl.BlockSpec((None, rows, d), lambda b, r: (b, jnp.minimum(r, nxb - 1), 0)),
            pl.BlockSpec((None, N_META, d), lambda b, r: (b, jnp.clip(r * per16 - 1, 0, seq // N_META - 1), 0)),
            pl.BlockSpec((N_META, d), lambda b, r: (0, 0)),
            pl.BlockSpec((1, d), lambda b, r: (0, 0)),
        ],
        out_specs=[
            pl.BlockSpec((rows, d), lambda b, r: (b * nrb + r, 0)),
            pl.BlockSpec((rows, d), lambda b, r: (b * nrb + r, 0)),
            pl.BlockSpec((rows, LANES), lambda b, r: (b * nrb + r, 0)),
        ],
        out_shape=[jax.ShapeDtypeStruct((t, d), F32), jax.ShapeDtypeStruct((t, d), BF16),
                   jax.ShapeDtypeStruct((t, LANES), F32)],
        compiler_params=_cparams(("parallel", "parallel"), 4 * rows * d * (4 + 4 + 4 + 2) + (8 << 20)),
        name="embed_prep",
    )(x, x, meta_tokens.astype(x.dtype), gain.reshape(1, d))


def _tile0_col(i, j):
    return jnp.where(i == 0, j, 0)


def _skip0_col(i, j):
    return jnp.where(i == 0, 0, j)


def _ffn_up_math(x, r, wg, wu):
    g = jnp.dot(x, wg, preferred_element_type=F32) * r
    u = jnp.dot(x, wu, preferred_element_type=F32) * r
    return (_silu(g) * u * 0.5).astype(BF16)


def _ffn_up_first_kernel(x_ref, ssq_ref, wg_ref, wu_ref, o_ref, wgb_ref, wub_ref):
    wg = wg_ref[...].astype(BF16)
    wu = wu_ref[...].astype(BF16)
    wgb_ref[...] = wg
    wub_ref[...] = wu
    o_ref[...] = _ffn_up_math(x_ref[...], _row_rscale(ssq_ref[...]), wg, wu)


def _ffn_up_rest_kernel(x_ref, ssq_ref, wg_ref, wu_ref, o0_ref, o_ref):
    i = pl.program_id(0)

    @pl.when(i == 0)
    def _():
        o_ref[...] = o0_ref[...]

    @pl.when(i > 0)
    def _():
        o_ref[...] = _ffn_up_math(x_ref[...], _row_rscale(ssq_ref[...]), wg_ref[...], wu_ref[...])


def _ffn_up(hg, ssq, w_gate_up, layer):
    t, d = hg.shape
    tm = _pick_tile(t, (1280, 640, 256, 128))
    tn1 = 256
    nj1 = D_FF // tn1
    vmem1 = 2 * (tm * d * 2 + tm * LANES * 4 + 2 * d * tn1 * (4 + 2) + tm * tn1 * 2) + 6 * tm * tn1 * 4 + (4 << 20)
    a0, wgb, wub = pl.pallas_call(
        _ffn_up_first_kernel,
        grid=(nj1,),
        in_specs=[
            pl.BlockSpec((tm, d), lambda j: (0, 0)),
            pl.BlockSpec((tm, LANES), lambda j: (0, 0)),
            pl.BlockSpec((None, d, tn1), lambda j: (layer, 0, j)),
            pl.BlockSpec((None, d, tn1), lambda j: (layer, 0, j + nj1)),
        ],
        out_specs=[
            pl.BlockSpec((tm, tn1), lambda j: (0, j)),
            pl.BlockSpec((d, tn1), lambda j: (0, j)),
            pl.BlockSpec((d, tn1), lambda j: (0, j)),
        ],
        out_shape=[jax.ShapeDtypeStruct((tm, D_FF), BF16), jax.ShapeDtypeStruct((d, D_FF), BF16),
                   jax.ShapeDtypeStruct((d, D_FF), BF16)],
        compiler_params=_cparams(("arbitrary",), vmem1),
        name="ffn_up_first",
    )(hg, ssq, w_gate_up, w_gate_up)
    if t == tm:
        return a0
    tn = 512
    vmem = 2 * (tm * d * 2 + tm * LANES * 4 + 2 * d * tn * 2 + 2 * tm * tn * 2) + 6 * tm * tn * 4 + (4 << 20)
    return pl.pallas_call(
        _ffn_up_rest_kernel,
        grid=(t // tm, D_FF // tn),
        in_specs=[
            pl.BlockSpec((tm, d), lambda i, j: (i, 0)),
            pl.BlockSpec((tm, LANES), lambda i, j: (i, 0)),
            pl.BlockSpec((d, tn), lambda i, j: (0, _skip0_col(i, j))),
            pl.BlockSpec((d, tn), lambda i, j: (0, _skip0_col(i, j))),
            pl.BlockSpec((tm, tn), lambda i, j: (0, _tile0_col(i, j))),
        ],
        out_specs=pl.BlockSpec((tm, tn), lambda i, j: (i, j)),
        out_shape=jax.ShapeDtypeStruct((t, D_FF), BF16),
        compiler_params=_cparams(("parallel", "arbitrary"), vmem),
        name="ffn_up",
    )(hg, ssq, wgb, wub, a0)


def _mm_res_store(hn, gn_ref, j, o_ref, hg_ref, ssq_ref):
    o_ref[...] = hn
    if hg_ref is None:
        return
    hg_ref[...] = (hn * gn_ref[...]).astype(hg_ref.dtype)
    part = _lane_partial_sumsq(hn)

    @pl.when(j == 0)
    def _():
        ssq_ref[...] = part

    @pl.when(j > 0)
    def _():
        ssq_ref[...] += part


def _mm_res_first_kernel(a_ref, w_ref, h_ref, gn_ref, o_ref, wb_ref, hg_ref=None, ssq_ref=None):
    w = w_ref[...].astype(BF16)
    wb_ref[...] = w
    hn = h_ref[...] + jnp.dot(a_ref[...], w, preferred_element_type=F32)
    _mm_res_store(hn, gn_ref, pl.program_id(0), o_ref, hg_ref, ssq_ref)


def _mm_res_rest_kernel(a_ref, w_ref, h_ref, gn_ref, o0_ref, hg0_ref, ssq0_ref, o_ref, hg_ref, ssq_ref):
    i = pl.program_id(0)

    @pl.when(i == 0)
    def _():
        o_ref[...] = o0_ref[...]
        if hg_ref is not None:
            hg_ref[...] = hg0_ref[...]
            ssq_ref[...] = ssq0_ref[...]

    @pl.when(i > 0)
    def _():
        hn = h_ref[...] + jnp.dot(a_ref[...], w_ref[...], preferred_element_type=F32)
        _mm_res_store(hn, gn_ref, pl.program_id(1), o_ref, hg_ref, ssq_ref)


def _mm_res_rest_plain_kernel(a_ref, w_ref, h_ref, gn_ref, o0_ref, o_ref):
    _mm_res_rest_kernel(a_ref, w_ref, h_ref, gn_ref, o0_ref, None, None, o_ref, None, None)


def _mm_res(a, w, layer, h, next_gain):
    t, kdim = a.shape
    n = w.shape[2]
    emit = next_gain is not None
    gn = (next_gain if emit else jnp.ones((n,), F32)).reshape(1, n)
    tm = _pick_tile(t, (640, 256, 128))
    tn1 = 256
    blk = lambda shape, imap: pl.BlockSpec(shape, imap)
    vmem1 = 2 * (tm * kdim * 2 + kdim * tn1 * (4 + 2) + tm * tn1 * (4 + 4 + 2) + tm * LANES * 4) + 4 * tm * tn1 * 4 + (4 << 20)
    out_specs1 = [blk((tm, tn1), lambda j: (0, j)), blk((kdim, tn1), lambda j: (0, j))]
    out_shape1 = [jax.ShapeDtypeStruct((tm, n), F32), jax.ShapeDtypeStruct((kdim, n), BF16)]
    if emit:
        out_specs1 += [blk((tm, tn1), lambda j: (0, j)), blk((tm, LANES), lambda j: (0, 0))]
        out_shape1 += [jax.ShapeDtypeStruct((tm, n), BF16), jax.ShapeDtypeStruct((tm, LANES), F32)]
    first = pl.pallas_call(
        _mm_res_first_kernel,
        grid=(n // tn1,),
        in_specs=[blk((tm, kdim), lambda j: (0, 0)), blk((None, kdim, tn1), lambda j: (layer, 0, j)),
                  blk((tm, tn1), lambda j: (0, j)), blk((1, tn1), lambda j: (0, j))],
        out_specs=out_specs1,
        out_shape=out_shape1,
        compiler_params=_cparams(("arbitrary",), vmem1),
        name="mm_res_first",
    )(a, w, h, gn)
    if t == tm:
        return (first[0], first[2], first[3]) if emit else first[0]
    tn = 512 if kdim > 4096 else 1024
    vmem = (2 * (tm * kdim * 2 + kdim * tn * 2 + tm * tn * (4 + 4 + 2 + 4 + 2) + 2 * tm * LANES * 4)
            + 3 * tm * tn * 4 + (4 << 20))
    in_specs = [blk((tm, kdim), lambda i, j: (i, 0)), blk((kdim, tn), lambda i, j: (0, _skip0_col(i, j))),
                blk((tm, tn), lambda i, j: (i, _skip0_col(i, j))), blk((1, tn), lambda i, j: (0, j)),
                blk((tm, tn), lambda i, j: (0, _tile0_col(i, j)))]
    out_specs = [blk((tm, tn), lambda i, j: (i, j))]
    out_shape = [jax.ShapeDtypeStruct((t, n), F32)]
    args = [a, first[1], h, gn, first[0]]
    if emit:
        in_specs += [blk((tm, tn), lambda i, j: (0, _tile0_col(i, j))), blk((tm, LANES), lambda i, j: (0, 0))]
        out_specs += [blk((tm, tn), lambda i, j: (i, j)), blk((tm, LANES), lambda i, j: (i, 0))]
        out_shape += [jax.ShapeDtypeStruct((t, n), BF16), jax.ShapeDtypeStruct((t, LANES), F32)]
        args += [first[2], first[3]]
    rest = pl.pallas_call(
        _mm_res_rest_kernel if emit else _mm_res_rest_plain_kernel,
        grid=(t // tm, n // tn),
        in_specs=in_specs,
        out_specs=out_specs,
        out_shape=out_shape,
        compiler_params=_cparams(("parallel", "arbitrary"), vmem),
        name="mm_res",
    )(*args)
    return tuple(rest) if emit else rest[0]


def _inproj_kernel(x_ref, ssq_ref, w_ref, o_ref):
    r = _row_rscale(ssq_ref[...])
    o_ref[...] = (jnp.dot(x_ref[...], w_ref[...], preferred_element_type=F32) * r).astype(o_ref.dtype)


def _inproj_small_kernel(x_ref, ssq_ref, w_ref, ws_ref, o_ref, os_ref):
    _inproj_kernel(x_ref, ssq_ref, w_ref, o_ref)

    @pl.when(pl.program_id(1) == 0)
    def _():
        os_ref[...] = jnp.dot(x_ref[...], ws_ref[...], preferred_element_type=F32) * _row_rscale(ssq_ref[...])


def _inproj(hg, ssq, w, w_small=None):
    t, d = hg.shape
    n = w.shape[1]
    tm = _pick_tile(t, (1280, 640, 256, 128))
    tn = 1024
    vmem = (2 * (tm * d * 2 + tm * LANES * 4 + d * tn * 2 + tm * tn * 2 + d * LANES * 2 + tm * LANES * 4)
            + 2 * tm * tn * 4 + (4 << 20))
    in_specs = [
        pl.BlockSpec((tm, d), lambda i, j: (i, 0)),
        pl.BlockSpec((tm, LANES), lambda i, j: (i, 0)),
        pl.BlockSpec((d, tn), lambda i, j: (0, j)),
    ]
    out_specs = [pl.BlockSpec((tm, tn), lambda i, j: (i, j))]
    out_shape = [jax.ShapeDtypeStruct((t, n), BF16)]
    args = [hg, ssq, w]
    if w_small is not None:
        in_specs.append(pl.BlockSpec((d, LANES), lambda i, j: (0, 0)))
        out_specs.append(pl.BlockSpec((tm, LANES), lambda i, j: (i, 0)))
        out_shape.append(jax.ShapeDtypeStruct((t, LANES), F32))
        args.append(w_small)
    out = pl.pallas_call(
        _inproj_kernel if w_small is None else _inproj_small_kernel,
        grid=(t // tm, n // tn),
        in_specs=in_specs,
        out_specs=out_specs,
        out_shape=out_shape,
        compiler_params=_cparams(("parallel", "arbitrary"), vmem),
        name="inproj",
    )(*args)
    return out[0] if w_small is None else tuple(out)


def _branch_math(y, o, wm, wg, gm, gg, bm, bg):
    br_m = jnp.dot(y, wm, preferred_element_type=F32)
    br_g = jnp.dot(o, wg, preferred_element_type=F32)
    gate_m = jax.nn.sigmoid(gm.astype(F32) + bm)
    gate_g = jax.nn.sigmoid(gg.astype(F32) + bg)
    return (gate_m * br_m + gate_g * br_g).astype(BF16)


def _branch_first_kernel(y_ref, o_ref, wm_ref, wg_ref, gm_ref, gg_ref, bm_ref, bg_ref, out_ref, wmb_ref, wgb_ref):
    wm = wm_ref[...].astype(BF16)
    wg = wg_ref[...].astype(BF16)
    wmb_ref[...] = wm
    wgb_ref[...] = wg
    out_ref[...] = _branch_math(y_ref[...], o_ref[...], wm, wg, gm_ref[...], gg_ref[...], bm_ref[...], bg_ref[...])


def _branch_rest_kernel(y_ref, o_ref, wm_ref, wg_ref, gm_ref, gg_ref, bm_ref, bg_ref, out0_ref, out_ref):
    i = pl.program_id(0)

    @pl.when(i == 0)
    def _():
        out_ref[...] = out0_ref[...]

    @pl.when(i > 0)
    def _():
        out_ref[...] = _branch_math(y_ref[...], o_ref[...], wm_ref[...], wg_ref[...], gm_ref[...], gg_ref[...],
                                    bm_ref[...], bg_ref[...])


def _branch_merge(y, o, w_m, w_g, layer, proj, gate_b):
    t, d = y.shape
    tm = _pick_tile(t, (640, 256, 128))
    gb = gate_b.reshape(1, 2 * d)
    tn1 = 256
    nj1 = d // tn1
    vmem1 = 2 * (2 * tm * d * 2 + 2 * d * tn1 * (4 + 2) + 3 * tm * tn1 * 2) + 6 * tm * tn1 * 4 + (4 << 20)
    out0, wmb, wgb = pl.pallas_call(
        _branch_first_kernel,
        grid=(nj1,),
        in_specs=[
            pl.BlockSpec((tm, d), lambda j: (0, 0)),
            pl.BlockSpec((tm, d), lambda j: (0, 0)),
            pl.BlockSpec((None, d, tn1), lambda j: (layer, 0, j)),
            pl.BlockSpec((None, d, tn1), lambda j: (layer, 0, j)),
            pl.BlockSpec((tm, tn1), lambda j: (0, j + PC_GM // tn1)),
            pl.BlockSpec((tm, tn1), lambda j: (0, j + PC_GG // tn1)),
            pl.BlockSpec((1, tn1), lambda j: (0, j)),
            pl.BlockSpec((1, tn1), lambda j: (0, j + nj1)),
        ],
        out_specs=[
            pl.BlockSpec((tm, tn1), lambda j: (0, j)),
            pl.BlockSpec((d, tn1), lambda j: (0, j)),
            pl.BlockSpec((d, tn1), lambda j: (0, j)),
        ],
        out_shape=[jax.ShapeDtypeStruct((tm, d), BF16), jax.ShapeDtypeStruct((d, d), BF16),
                   jax.ShapeDtypeStruct((d, d), BF16)],
        compiler_params=_cparams(("arbitrary",), vmem1),
        name="branch_merge_first",
    )(y, o, w_m, w_g, proj, proj, gb, gb)
    if t == tm:
        return out0
    tn = 512
    jm, jg = PC_GM // tn, PC_GG // tn
    nj = d // tn
    vmem = 2 * (2 * tm * d * 2 + 2 * d * tn * 2 + 4 * tm * tn * 2) + 6 * tm * tn * 4 + (4 << 20)
    return pl.pallas_call(
        _branch_rest_kernel,
        grid=(t // tm, nj),
        in_specs=[
            pl.BlockSpec((tm, d), lambda i, j: (i, 0)),
            pl.BlockSpec((tm, d), lambda i, j: (i, 0)),
            pl.BlockSpec((d, tn), lambda i, j: (0, _skip0_col(i, j))),
            pl.BlockSpec((d, tn), lambda i, j: (0, _skip0_col(i, j))),
            pl.BlockSpec((tm, tn), lambda i, j: (i, _skip0_col(i, j) + jm)),
            pl.BlockSpec((tm, tn), lambda i, j: (i, _skip0_col(i, j) + jg)),
            pl.BlockSpec((1, tn), lambda i, j: (0, j)),
            pl.BlockSpec((1, tn), lambda i, j: (0, j + nj)),
            pl.BlockSpec((tm, tn), lambda i, j: (0, _tile0_col(i, j))),
        ],
        out_specs=pl.BlockSpec((tm, tn), lambda i, j: (i, j)),
        out_shape=jax.ShapeDtypeStruct((t, d), BF16),
        compiler_params=_cparams(("parallel", "arbitrary"), vmem),
        name="branch_merge",
    )(y, o, wmb, wgb, proj, proj, gb, gb, out0)


def _conv_silu(x_ref, hist_ref, w_ref, b_ref, rows):
    cur = x_ref[...]
    full = jnp.concatenate([hist_ref[...], cur], axis=0)
    ri = lax.broadcasted_iota(jnp.int32, (3 * rows, 2 * rows), 0)
    ci = lax.broadcasted_iota(jnp.int32, (3 * rows, 2 * rows), 1)
    shift = jnp.where(ci == rows + (ri & (rows - 1)) - (ri // rows + 1), 1.0, 0.0).astype(BF16)
    delayed = jnp.dot(shift, full, preferred_element_type=F32)
    acc = w_ref[3:4, :] * cur.astype(F32)
    for k in range(1, 4):
        acc = acc + w_ref[3 - k:4 - k, :] * delayed[(k - 1) * rows:k * rows, :]
    if b_ref is not None:
        acc = acc + b_ref[...]
    hist_ref[...] = cur
    return _silu(acc)


def _ssd_kernel(xs_ref, bc_ref, z_ref, sm_ref, cwx_ref, cbx_ref, cwbc_ref, cbbc_ref, dtb_ref, alog_ref,
                drep_ref, nw_ref, e64_ref, e128_ref,
                y_ref,
                st_ref, xpad_ref, bcpad_ref, xact_ref, xbf_ref, xw_ref, bcact_ref, larep_ref, exprep_ref,
                lat_ref, dtt_ref, yacc_ref):
    q = SSM_CHUNK
    hg = SSM_HEADS // SSM_GROUPS
    gw = hg * SSM_HEAD_DIM

    @pl.when(pl.program_id(1) == 0)
    def _():
        st_ref[...] = jnp.zeros_like(st_ref)
        xpad_ref[...] = jnp.zeros_like(xpad_ref)
        bcpad_ref[...] = jnp.zeros_like(bcpad_ref)

    xact = _conv_silu(xs_ref, xpad_ref, cwx_ref, cbx_ref, q)
    xact_ref[...] = xact
    xbf_ref[...] = xact.astype(BF16)
    bcact_ref[...] = _conv_silu(bc_ref, bcpad_ref, cwbc_ref, cbbc_ref, q)

    row = lax.broadcasted_iota(jnp.int32, (q, q), 0)
    col = lax.broadcasted_iota(jnp.int32, (q, q), 1)
    causal = row >= col
    tril = jnp.where(causal, 1.0, 0.0).astype(BF16)

    head_lane = col < SSM_HEADS
    dt = jnp.where(head_lane, _softplus(sm_ref[...] + dtb_ref[...]), 0.0)
    a = -jnp.exp(alog_ref[...])
    la = _dot_split_rhs(tril, dt * a, 3)
    la_hi, la_lo = _two_term(la)
    la = la_hi.astype(F32) + la_lo.astype(F32)
    la_last = la[q - 1:q, :]
    larep_ref[...] = _replicate_dot(la, e128_ref[...])
    exprep_ref[...] = _replicate_dot(jnp.exp(la), e64_ref[...])
    to_end = jnp.exp(la_last - la) * dt
    xw_ref[...] = (xact * _replicate_dot(to_end, e64_ref[...])).astype(BF16)
    lat_ref[...] = la.T
    dtt_ref[...] = dt.T

    def one_group(g):
        c0 = pl.multiple_of(g * SSM_STATE, SSM_STATE)
        x0 = pl.multiple_of(g * gw, gw)
        bg = bcact_ref[:, pl.ds(c0, SSM_STATE)]
        cg = bcact_ref[:, pl.ds(SSM_GROUPS * SSM_STATE + c0, SSM_STATE)].astype(BF16)
        cb = _dot_nt(cg, bg.astype(BF16))
        st_g = st_ref[:, pl.ds(x0, gw)]
        dec_g = exprep_ref[:, pl.ds(x0, gw)]
        y_inter = jnp.dot(cg, st_g.astype(BF16), preferred_element_type=F32) * dec_g
        st_ref[:, pl.ds(x0, gw)] = st_g * dec_g[q - 1:q, :] + jnp.dot(
            bg.T.astype(BF16), xw_ref[:, pl.ds(x0, gw)], preferred_element_type=F32)
        for pp in range(hg // 2):
            xp0 = pl.multiple_of(x0 + pp * LANES, LANES)
            xpair = xbf_ref[:, pl.ds(xp0, LANES)]
            res = []
            for e in range(2):
                h = g * hg + pp * 2 + e
                seg = larep_ref[:, pl.ds(pl.multiple_of(h * q, q), q)] - lat_ref[pl.ds(h, 1), :]
                dec = jnp.where(causal, jnp.exp(seg), 0.0)
                lmat = (cb * dec * dtt_ref[pl.ds(h, 1), :]).astype(BF16)
                res.append(jnp.dot(lmat, xpair, preferred_element_type=F32))
            y_intra = jnp.where(col < SSM_HEAD_DIM, res[0], res[1])
            yacc_ref[:, pl.ds(xp0, LANES)] = y_intra + y_inter[:, pp * LANES:(pp + 1) * LANES]

    def groups_body(it, carry):
        for e in range(SSM_GROUPS_PER_ITER):
            one_group(it * SSM_GROUPS_PER_ITER + e)
        return carry

    lax.fori_loop(0, SSM_GROUPS // SSM_GROUPS_PER_ITER, groups_body, 0)

    z = z_ref[...].astype(F32)
    y = (yacc_ref[...] + drep_ref[...] * xact_ref[...]) * _silu(z)
    for g in range(SSM_GROUPS):
        yg = y[:, g * gw:(g + 1) * gw]
        ms = jnp.mean(yg * yg, axis=-1, keepdims=True)
        y_ref[:, g * gw:(g + 1) * gw] = (yg * lax.rsqrt(ms + NORM_EPS) * nw_ref[:, g * gw:(g + 1) * gw]).astype(y_ref.dtype)


def _ssd(proj, small, conv_w, conv_b, dt_bias, a_log, d_skip, norm_w, bsz, tb):
    q = SSM_CHUNK
    nc = tb // q
    t = bsz * tb
    pad = LANES - SSM_HEADS
    dtb = jnp.pad(dt_bias, (0, pad)).reshape(1, LANES)
    alog = jnp.pad(a_log, (0, pad)).reshape(1, LANES)
    drep = jnp.repeat(d_skip, SSM_HEAD_DIM).reshape(1, D_SSM)
    r = jnp.arange(2 * LANES)[:, None] % LANES
    e64 = (jnp.arange(D_SSM)[None, :] // SSM_HEAD_DIM == r).astype(BF16)
    e128 = (jnp.arange(SSM_HEADS * q)[None, :] // q == r).astype(BF16)
    const = lambda b, c: (0, 0)
    rowblk = lambda off: (lambda b, c: (b * nc + c, off))
    return pl.pallas_call(
        _ssd_kernel,
        grid=(bsz, nc),
        in_specs=[
            pl.BlockSpec((q, D_SSM), rowblk(PA_XS // D_SSM)),
            pl.BlockSpec((q, D_BC), rowblk(PA_BC // D_BC)),
            pl.BlockSpec((q, D_SSM), rowblk(PA_ZM // D_SSM)),
            pl.BlockSpec((q, LANES), rowblk(0)),
            pl.BlockSpec((SSM_CONV, D_SSM), const),
            pl.BlockSpec((1, D_SSM), const),
            pl.BlockSpec((SSM_CONV, D_BC), const),
            pl.BlockSpec((1, D_BC), const),
            pl.BlockSpec((1, LANES), const),
            pl.BlockSpec((1, LANES), const),
            pl.BlockSpec((1, D_SSM), const),
            pl.BlockSpec((1, D_SSM), const),
            pl.BlockSpec((2 * LANES, D_SSM), const),
            pl.BlockSpec((2 * LANES, SSM_HEADS * q), const),
        ],
        out_specs=pl.BlockSpec((q, D_SSM), rowblk(0)),
        out_shape=jax.ShapeDtypeStruct((t, D_SSM), BF16),
        scratch_shapes=[
            pltpu.VMEM((SSM_STATE, D_SSM), F32),
            pltpu.VMEM((q, D_SSM), BF16),
            pltpu.VMEM((q, D_BC), BF16),
            pltpu.VMEM((q, D_SSM), F32),
            pltpu.VMEM((q, D_SSM), BF16),
            pltpu.VMEM((q, D_SSM), BF16),
            pltpu.VMEM((q, D_BC), F32),
            pltpu.VMEM((q, SSM_HEADS * q), F32),
            pltpu.VMEM((q, D_SSM), F32),
            pltpu.VMEM((LANES, q), F32),
            pltpu.VMEM((LANES, q), F32),
            pltpu.VMEM((q, D_SSM), F32),
        ],
        compiler_params=_cparams(("parallel", "arbitrary"), 56 << 20),
        name="ssd_scan",
    )(proj, proj, proj, small, conv_w[:, :D_SSM], conv_b[:D_SSM].reshape(1, D_SSM), conv_w[:, D_SSM:],
      conv_b[D_SSM:].reshape(1, D_BC), dtb, alog, drep, norm_w.reshape(1, D_SSM), e64, e128)


def _unit_lower_inverse_minus_eye(ms, row, col):
    blk16 = (row ^ col) < 16
    blk32 = (row ^ col) < 32
    ps = [jnp.where(blk16, m, 0.0) for m in ms]
    ns = [-p for p in ps]
    for _ in range(3):
        ps = [_bdot(p, p) for p in ps]
        ns = [n + p + _bdot(n, p) for n, p in zip(ns, ps)]
    for sel in (blk32 & ~blk16, ~blk32):
        offs = [jnp.where(sel, m, 0.0) for m in ms]
        us = [off + _bdot(n, off) for n, off in zip(ns, offs)]
        ns = [n - (u + _bdot(u, n)) for n, u in zip(ns, us)]
    return ns


def _gdn_kernel(q_ref, k_ref, v_ref, z_ref, sm_ref, cwq_ref, cwk_ref, cwv_ref, dtb_ref, alog_ref, nw_ref,
                eg_ref, eb_ref,
                o_ref,
                s_ref, qpad_ref, kpad_ref, vpad_ref, qn_ref, kn_ref, vact_ref, gcrep_ref, betarep_ref, xt_ref):
    c = GDN_CHUNK
    rep = GDN_V_HEADS // GDN_QK_HEADS

    @pl.when(pl.program_id(1) == 0)
    def _():
        s_ref[...] = jnp.zeros_like(s_ref)
        qpad_ref[...] = jnp.zeros_like(qpad_ref)
        kpad_ref[...] = jnp.zeros_like(kpad_ref)
        vpad_ref[...] = jnp.zeros_like(vpad_ref)

    qact = _conv_silu(q_ref, qpad_ref, cwq_ref, None, c)
    kact = _conv_silu(k_ref, kpad_ref, cwk_ref, None, c)
    vact_ref[...] = _conv_silu(v_ref, vpad_ref, cwv_ref, None, c)
    for h in range(GDN_QK_HEADS):
        sl = slice(h * GDN_DK, (h + 1) * GDN_DK)
        qh = qact[:, sl]
        kh = kact[:, sl]
        qn_ref[:, sl] = qh * lax.rsqrt(jnp.sum(qh * qh, axis=-1, keepdims=True) + NORM_EPS) * (GDN_DK ** -0.5)
        kn_ref[:, sl] = kh * lax.rsqrt(jnp.sum(kh * kh, axis=-1, keepdims=True) + NORM_EPS)

    row = lax.broadcasted_iota(jnp.int32, (c, c), 0)
    col = lax.broadcasted_iota(jnp.int32, (c, c), 1)
    incl = row >= col
    strict = row > col
    tril = jnp.where(incl, 1.0, 0.0).astype(BF16)

    sm = sm_ref[...]
    lane = lax.broadcasted_iota(jnp.int32, (c, LANES), 1)
    beta = jnp.where((lane >= 64) & (lane < 96), jax.nn.sigmoid(sm), 0.0)
    g = jnp.where(lane >= 96, -jnp.exp(alog_ref[...]) * _softplus(sm + dtb_ref[...]), 0.0)
    gc = _dot_split_rhs(tril, g, 3)
    gc_hi, gc_lo = _two_term(gc)
    gc = gc_hi.astype(F32) + gc_lo.astype(F32)
    gcrep_ref[...] = _replicate_dot(gc, eg_ref[...])
    betarep_ref[...] = _replicate_dot(beta, eb_ref[...])
    xt_ref[...] = jnp.concatenate([gc, jnp.zeros_like(gc)], axis=0).T

    def heads_body(it, carry):
        nv = GDN_HEADS_PER_ITER
        vheads = [it * nv + e for e in range(nv)]
        chs = [pl.multiple_of(h * GDN_DK, GDN_DK) for h in vheads]
        qhs, khs, kkts, qkts = [], [], [], []
        for e in range(nv // rep):
            cq = pl.multiple_of((it * (nv // rep) + e) * GDN_DK, GDN_DK)
            qh = qn_ref[:, pl.ds(cq, GDN_DK)]
            kh = kn_ref[:, pl.ds(cq, GDN_DK)]
            kb = kh.astype(BF16)
            kkt = _dot_nt(kb, kb)
            qkt = _dot_nt(qh.astype(BF16), kb)
            for _ in range(rep):
                qhs.append(qh)
                khs.append(kh)
                kkts.append(kkt)
                qkts.append(qkt)
        gcols = [gcrep_ref[:, pl.ds(ch, GDN_DK)] for ch in chs]
        bcols = [betarep_ref[:, pl.ds(ch, GDN_DK)] for ch in chs]
        grows = [xt_ref[pl.ds(96 + h, 1), :][:, :c] for h in vheads]
        gams = [jnp.where(incl, jnp.exp(gcol[:, :c] - grow), 0.0) for gcol, grow in zip(gcols, grows)]
        ms = [jnp.where(strict, kkt * gam * bcol[:, :c], 0.0) for kkt, gam, bcol in zip(kkts, gams, bcols)]
        ns = _unit_lower_inverse_minus_eye(ms, row, col)
        egs = [jnp.exp(gcol) for gcol in gcols]
        rhss = [jnp.concatenate([vact_ref[:, pl.ds(ch, GDN_DK)] * bcol, kh * (bcol * eg)], axis=1)
                for ch, bcol, kh, eg in zip(chs, bcols, khs, egs)]
        sols = [rhs + _bdot(n, rhs) for n, rhs in zip(ns, rhss)]
        ss = [s_ref[h] for h in vheads]
        sbs = [s.astype(BF16) for s in ss]
        vbs = [(sol[:, :GDN_DK] - jnp.dot(sol[:, GDN_DK:].astype(BF16), sb, preferred_element_type=F32)).astype(BF16)
               for sol, sb in zip(sols, sbs)]
        os_ = [jnp.dot((qh * eg).astype(BF16), sb, preferred_element_type=F32)
               + jnp.dot((qkt * gam).astype(BF16), vb, preferred_element_type=F32)
               for qh, eg, sb, qkt, gam, vb in zip(qhs, egs, sbs, qkts, gams, vbs)]
        for h, s, gcol, kh, vb in zip(vheads, ss, gcols, khs, vbs):
            glast = gcol[c - 1:c, :]
            kdec = kh * jnp.exp(glast - gcol)
            s_ref[h] = s * jnp.exp(glast) + lax.dot_general(
                kdec.astype(BF16), vb, (((0,), (0,)), ((), ())), preferred_element_type=F32)
        for ch, o in zip(chs, os_):
            msq = jnp.mean(o * o, axis=-1, keepdims=True)
            z = z_ref[:, pl.ds(ch, GDN_DK)].astype(F32)
            o_ref[:, pl.ds(ch, GDN_DK)] = (o * lax.rsqrt(msq + NORM_EPS) * nw_ref[...] * _silu(z)).astype(o_ref.dtype)
        return carry

    lax.fori_loop(0, GDN_V_HEADS // GDN_HEADS_PER_ITER, heads_body, 0)


def _gdn(proj, small, conv_w, dt_bias, a_log, norm_w, bsz, tb):
    c = GDN_CHUNK
    nc = tb // c
    t = bsz * tb
    dtb = jnp.pad(dt_bias, (LANES - GDN_V_HEADS, 0)).reshape(1, LANES)
    alog = jnp.pad(a_log, (LANES - GDN_V_HEADS, 0)).reshape(1, LANES)
    r = jnp.arange(2 * LANES)[:, None] % LANES
    head_of_col = jnp.arange(D_V)[None, :] // GDN_DK
    eg = (r == 96 + head_of_col).astype(BF16)
    eb = (r == 64 + head_of_col).astype(BF16)
    const = lambda b, i: (0, 0)
    rowblk = lambda off: (lambda b, i: (b * nc + i, off))
    return pl.pallas_call(
        _gdn_kernel,
        grid=(bsz, nc),
        in_specs=[
            pl.BlockSpec((c, D_QK), rowblk(PB_Q // D_QK)),
            pl.BlockSpec((c, D_QK), rowblk(PB_K // D_QK)),
            pl.BlockSpec((c, D_V), rowblk(PB_V // D_V)),
            pl.BlockSpec((c, D_V), rowblk(PB_ZG // D_V)),
            pl.BlockSpec((c, LANES), rowblk(0)),
            pl.BlockSpec((GDN_CONV, D_QK), const),
            pl.BlockSpec((GDN_CONV, D_QK), const),
            pl.BlockSpec((GDN_CONV, D_V), const),
            pl.BlockSpec((1, LANES), const),
            pl.BlockSpec((1, LANES), const),
            pl.BlockSpec((1, GDN_DK), const),
            pl.BlockSpec((2 * LANES, D_V), const),
            pl.BlockSpec((2 * LANES, D_V), const),
        ],
        out_specs=pl.BlockSpec((c, D_V), rowblk(0)),
        out_shape=jax.ShapeDtypeStruct((t, D_V), BF16),
        scratch_shapes=[
            pltpu.VMEM((GDN_V_HEADS, GDN_DK, GDN_DK), F32),
            pltpu.VMEM((c, D_QK), BF16),
            pltpu.VMEM((c, D_QK), BF16),
            pltpu.VMEM((c, D_V), BF16),
            pltpu.VMEM((c, D_QK), F32),
            pltpu.VMEM((c, D_QK), F32),
            pltpu.VMEM((c, D_V), F32),
            pltpu.VMEM((c, D_V), F32),
            pltpu.VMEM((c, D_V), F32),
            pltpu.VMEM((LANES, LANES), F32),
        ],
        compiler_params=_cparams(("parallel", "arbitrary"), 48 << 20),
        name="gdn_scan",
    )(proj, proj, proj, proj, small, conv_w[:, :D_QK], conv_w[:, D_QK:2 * D_QK], conv_w[:, 2 * D_QK:],
      dtb, alog, norm_w.reshape(1, GDN_DK), eg, eb)


def _final_norm_kernel(a_ref, b_ref, g_ref, o_ref):
    x = jnp.concatenate([a_ref[N_META:, :], b_ref[...]], axis=0)
    ms = jnp.mean(x * x, axis=-1, keepdims=True)
    o_ref[...] = (x * lax.rsqrt(ms + NORM_EPS) * g_ref[...]).astype(o_ref.dtype)


def _final_norm(h, gain, bsz, seq, tb):
    d = h.shape[1]
    r = 128
    nb = seq // r
    return pl.pallas_call(
        _final_norm_kernel,
        grid=(bsz, nb),
        in_specs=[
            pl.BlockSpec((r, d), lambda b, i: (b * (tb // r) + i, 0)),
            pl.BlockSpec((N_META, d), lambda b, i: ((b * tb + (i + 1) * r) // N_META, 0)),
            pl.BlockSpec((1, d), lambda b, i: (0, 0)),
        ],
        out_specs=pl.BlockSpec((None, r, d), lambda b, i: (b, i, 0)),
        out_shape=jax.ShapeDtypeStruct((bsz, seq, d), F32),
        compiler_params=_cparams(("parallel", "parallel"), 4 * r * d * 4 * 2 + (8 << 20)),
        name="final_norm",
    )(h, h, gain.reshape(1, d))


def _split_in_proj(w_in):
    a_end = 2 * D_SSM + D_BC
    b_start = a_end + SSM_HEADS
    b_end = b_start + 2 * D_QK + 2 * D_V
    c_start = b_end + 2 * GDN_V_HEADS
    small = jnp.concatenate([w_in[:, a_end:b_start], w_in[:, b_end:c_start]], axis=1)
    return (w_in[:, :a_end].astype(BF16), w_in[:, b_start:b_end].astype(BF16), w_in[:, c_start:].astype(BF16),
            small.astype(BF16))


def kernel(x, meta_tokens, ffn1_norm, ffn1_w_gate_up, ffn1_w_down, mix_norm, w_in, ssm_conv_w, ssm_conv_b,
           ssm_dt_bias, ssm_a_log, ssm_d, ssm_norm, ssm_w_out, gdn_conv_w, gdn_dt_bias, gdn_a_log, gdn_norm,
           gdn_w_out, gate_b, w_o, ffn2_norm, ffn2_w_gate_up, ffn2_w_down, final_norm):
    bsz, seq, d = x.shape
    ltot = seq + N_META
    tb = -(-ltot // SSM_CHUNK) * SSM_CHUNK
    depth = w_in.shape[0]
    h, hg, ssq = _embed_prep(x, meta_tokens, ffn1_norm[0], tb)
    for i in range(depth):
        a = _ffn_up(hg, ssq, ffn1_w_gate_up, i)
        h, hg, ssq = _mm_res(a, ffn1_w_down, i, h, mix_norm[i])
        w_a, w_b, w_c, w_small = _split_in_proj(w_in[i])
        proj_a, small = _inproj(hg, ssq, w_a, w_small)
        proj_b = _inproj(hg, ssq, w_b)
        proj_c = _inproj(hg, ssq, w_c)
        y = _ssd(proj_a, small, ssm_conv_w[i], ssm_conv_b[i], ssm_dt_bias[i], ssm_a_log[i], ssm_d[i], ssm_norm[i],
                 bsz, tb)
        o = _gdn(proj_b, small, gdn_conv_w[i], gdn_dt_bias[i], gdn_a_log[i], gdn_norm[i], bsz, tb)
        merged = _branch_merge(y, o, ssm_w_out, gdn_w_out, i, proj_c, gate_b[i])
        h, hg, ssq = _mm_res(merged, w_o, i, h, ffn2_norm[i])
        a = _ffn_up(hg, ssq, ffn2_w_gate_up, i)
        if i + 1 < depth:
            h, hg, ssq = _mm_res(a, ffn2_w_down, i, h, ffn1_norm[i + 1])
        else:
            h = _mm_res(a, ffn2_w_down, i, h, None)
    return _final_norm(h, final_norm, bsz, seq, tb)
```
